```python
import jax, jax.numpy as jnp
from jax import lax
import numpy as np

D_MODEL = 1024
BATCH = 4
SEQ = 4096
DEPTH = 1
DEC_BATCH = 32
DEC_SEQ = 1
PAST_LEN = 16384
PAGE_SIZE = 128

N_HEADS = 8
HEAD_DIM = 64
ATTN_W = N_HEADS * HEAD_DIM
MOBA_BLOCK = 256
MOBA_TOPK = 3
Q_CHUNK = 32
ROPE_THETA = 10000.0
LRU_W = D_MODEL
LRU_BLOCKS = 4
LRU_BW = LRU_W // LRU_BLOCKS
CONV_W = 4
LRU_C = 8.0
N_EXPERTS = 32
TOP_K = 4
D_EXPERT = D_MODEL
SWIGLU_LIMIT = 7.0
SWIGLU_ALPHA = 1.702
MOE_BLOCK = 128
NORM_EPS = 1e-6
IN_SPLITS = (ATTN_W, 2 * ATTN_W, 3 * ATTN_W, 3 * ATTN_W + LRU_W, 3 * ATTN_W + 2 * LRU_W,
             3 * ATTN_W + 2 * LRU_W + D_MODEL)
IN_COLS = 3 * ATTN_W + 2 * LRU_W + 2 * D_MODEL

kernel_name = 'griffin_moba_moe_decode_step'

F32 = jnp.float32


def rms_norm(x, g):
    xf = x.astype(F32)
    out = xf * lax.rsqrt(jnp.mean(xf * xf, axis=-1, keepdims=True) + NORM_EPS) * g.astype(F32)
    return out.astype(x.dtype)


def rope(x, pos):
    half = HEAD_DIM // 2
    inv = ROPE_THETA ** (-jnp.arange(half, dtype=F32) * 2.0 / HEAD_DIM)
    ang = pos.astype(F32)[:, None] * inv[None, :]
    cos = jnp.cos(ang)[None, :, None, :]
    sin = jnp.sin(ang)[None, :, None, :]
    xf = x.astype(F32)
    x1, x2 = xf[..., :half], xf[..., half:]
    return jnp.concatenate([x1 * cos - x2 * sin, x2 * cos + x1 * sin], axis=-1).astype(x.dtype)


def moba_attention(q, k, v, q_start):
    B, Sq, H, Dh = q.shape
    Sk = k.shape[1]
    nb = -(-Sk // MOBA_BLOCK)
    padw = ((0, 0), (0, nb * MOBA_BLOCK - Sk), (0, 0), (0, 0))
    k_blk = jnp.pad(k, padw).reshape(B, nb, MOBA_BLOCK, H, Dh)
    v_blk = jnp.pad(v, padw).reshape(B, nb, MOBA_BLOCK, H, Dh)
    k_mean = jnp.mean(k_blk.astype(F32), axis=2)
    n_sel = min(MOBA_TOPK, nb)
    qc = min(Q_CHUNK, Sq)
    n_chunks = -(-Sq // qc)
    q_pad = jnp.pad(q, ((0, 0), (0, n_chunks * qc - Sq), (0, 0), (0, 0)))
    q_chunks = q_pad.reshape(B, n_chunks, qc, H, Dh).transpose(1, 0, 2, 3, 4)
    t_chunks = (q_start + jnp.arange(n_chunks * qc, dtype=jnp.int32)).reshape(n_chunks, qc)
    b_ix = jnp.arange(B)[:, None, None, None]
    h_ix = jnp.arange(H)[None, None, :, None]
    blk_ids = jnp.arange(nb, dtype=jnp.int32)
    offs = jnp.arange(MOBA_BLOCK, dtype=jnp.int32)
    neg = jnp.finfo(F32).min
    scale = Dh ** -0.5

    def attend_chunk(args):
        qb, tb = args
        cur = tb // MOBA_BLOCK
        gate = jnp.einsum('bqhd,bnhd->bqhn', qb.astype(F32), k_mean)
        gate = jnp.where(blk_ids[None, None, None, :] < cur[None, :, None, None], gate, neg)
        _, top = lax.top_k(gate, n_sel)
        own = jnp.broadcast_to(cur[None, :, None, None], (B, qc, H, 1))
        idx = jnp.minimum(jnp.concatenate([top, own], axis=-1), nb - 1)
        live = jnp.concatenate([top < cur[None, :, None, None],
                                jnp.ones((B, qc, H, 1), dtype=bool)], axis=-1)
        kg = k_blk[b_ix, idx, :, h_ix]
        vg = v_blk[b_ix, idx, :, h_ix]
        s = jnp.einsum('bqhd,bqhnkd->bqhnk', qb, kg, preferred_element_type=F32) * scale
        kpos = idx[..., None] * MOBA_BLOCK + offs
        valid = live[..., None] & (kpos <= tb[None, :, None, None, None])
        s = jnp.where(valid, s, neg)
        p = jax.nn.softmax(s.reshape(B, qc, H, -1), axis=-1).reshape(s.shape)
        return jnp.einsum('bqhnk,bqhnkd->bqhd', p.astype(v.dtype), vg)

    out = lax.map(attend_chunk, (q_chunks, t_chunks))
    return out.transpose(1, 0, 2, 3, 4).reshape(B, n_chunks * qc, H, Dh)[:, :Sq]


def rglru_branch(xb, yb, pos, conv_state, h0, conv_w, conv_b, w_rg, b_rg, w_ig, b_ig, lru_lambda):
    B, S, W = xb.shape
    xpad = jnp.concatenate([conv_state.astype(xb.dtype), xb], axis=1)
    xc = conv_b
    for j in range(CONV_W):
        xc = xc + xpad[:, j:j + S] * conv_w[j]
    new_conv = xpad[:, S:]
    xcf = xc.astype(F32)
    xr = xcf.reshape(B, S, LRU_BLOCKS, LRU_BW)
    r = jax.nn.sigmoid(jnp.einsum('bsni,nij->bsnj', xr, w_rg.astype(F32)).reshape(B, S, W) + b_rg.astype(F32))
    i = jax.nn.sigmoid(jnp.einsum('bsni,nij->bsnj', xr, w_ig.astype(F32)).reshape(B, S, W) + b_ig.astype(F32))
    log_a = -LRU_C * r * jax.nn.softplus(-lru_lambda.astype(F32))
    a = jnp.exp(log_a)
    mult = jnp.sqrt(-jnp.expm1(2.0 * log_a))
    mult = jnp.where((pos == 0)[None, :, None], 1.0, mult)
    u = xcf * i * mult

    def step(h, au):
        a_t, u_t = au
        h = a_t * h + u_t
        return h, h

    h_last, hs = lax.scan(step, h0.astype(F32), (a.transpose(1, 0, 2), u.transpose(1, 0, 2)))
    out = hs.transpose(1, 0, 2) * jax.nn.gelu(yb.astype(F32))
    return out.astype(xb.dtype), h_last.astype(xb.dtype), new_conv


def moe_ffn(x2d, w_router, b_router, w_gu, b_gu, w_dn, b_dn):
    T, D = x2d.shape
    logits = (x2d @ w_router + b_router).astype(F32)
    top_v, top_e = lax.top_k(logits, TOP_K)
    gate_w = jax.nn.softmax(top_v, axis=-1)
    tk = T * TOP_K
    blk = max(1, min(MOE_BLOCK, tk // N_EXPERTS))
    n_blk = -(-tk // blk) + N_EXPERTS
    e_flat = top_e.reshape(tk)
    onehot = (e_flat[:, None] == jnp.arange(N_EXPERTS)[None, :]).astype(jnp.int32)
    counts = jnp.sum(onehot, axis=0)
    rank = jnp.sum(jnp.cumsum(onehot, axis=0) * onehot, axis=1) - 1
    padded = (counts + blk - 1) // blk * blk
    pad_end = jnp.cumsum(padded)
    pad_start = pad_end - padded
    dest = pad_start[e_flat] + rank
    tok = jnp.arange(tk) // TOP_K
    buf = jnp.zeros((n_blk * blk, D), x2d.dtype).at[dest].set(x2d[tok])
    blk_e = jnp.minimum(jnp.searchsorted(pad_end, jnp.arange(n_blk) * blk, side='right'), N_EXPERTS - 1)

    def expert_block(args):
        xb, e = args
        gu = xb @ w_gu[e] + b_gu[e]
        glu = jnp.minimum(gu[:, 0::2], SWIGLU_LIMIT)
        lin = jnp.clip(gu[:, 1::2], -SWIGLU_LIMIT, SWIGLU_LIMIT)
        act = glu * jax.nn.sigmoid(SWIGLU_ALPHA * glu) * (lin + 1.0)
        return act @ w_dn[e] + b_dn[e]

    out = lax.map(expert_block, (buf.reshape(n_blk, blk, D), blk_e)).reshape(n_blk * blk, D)
    return jnp.sum(out[dest].reshape(T, TOP_K, D) * gate_w[..., None].astype(x2d.dtype), axis=1)


def hybrid_layer(x, pos0, k_prefix, v_prefix, conv_state, h0, norm1_g, w_in, q_norm_g, k_norm_g,
                 conv_w, conv_b, w_rg, b_rg, w_ig, b_ig, lru_lambda, w_attn_o, w_lru_o, w_out,
                 norm2_g, w_router, b_router, w_gu, b_gu, w_dn, b_dn):
    B, S, D = x.shape
    pos = pos0 + jnp.arange(S, dtype=jnp.int32)
    xn = rms_norm(x, norm1_g)
    proj = xn @ w_in
    q, k, v, xb, yb, ga, gl = jnp.split(proj, IN_SPLITS, axis=-1)
    q = rope(rms_norm(q.reshape(B, S, N_HEADS, HEAD_DIM), q_norm_g), pos)
    k = rope(rms_norm(k.reshape(B, S, N_HEADS, HEAD_DIM), k_norm_g), pos)
    v = v.reshape(B, S, N_HEADS, HEAD_DIM)
    k_all = jnp.concatenate([k_prefix, k], axis=1)
    v_all = jnp.concatenate([v_prefix, v], axis=1)
    attn = moba_attention(q, k_all, v_all, pos0).reshape(B, S, ATTN_W)
    lru, h_last, new_conv = rglru_branch(xb, yb, pos, conv_state, h0, conv_w, conv_b,
                                         w_rg, b_rg, w_ig, b_ig, lru_lambda)
    mixed = jax.nn.sigmoid(ga) * (attn @ w_attn_o) + jax.nn.sigmoid(gl) * (lru @ w_lru_o)
    h = x + mixed @ w_out
    ff = moe_ffn(rms_norm(h, norm2_g).reshape(B * S, D), w_router, b_router, w_gu, b_gu, w_dn, b_dn)
    return h + ff.reshape(B, S, D), k, v, h_last, new_conv


def setup_inputs(seed: int = 0) -> dict:
    key = jax.random.key(seed)
    ks = jax.random.split(key, 32)
    n_pages = PAST_LEN // PAGE_SIZE
    n_used = DEC_BATCH * n_pages
    n_pool = n_used + max(1, n_used // 4)

    def nrm(k, shape, scale):
        return jax.random.normal(k, shape, F32) * scale

    def gain(k, n):
        return 1.0 + 0.05 * jax.random.normal(k, (DEPTH, n), F32)

    u = jax.random.uniform(ks[17], (DEPTH, LRU_W), F32, 0.9, 0.999)
    s = u ** (1.0 / LRU_C)
    lam = jnp.log(s) - jnp.log1p(-s)
    page_table = jax.random.permutation(ks[6], n_pool)[:n_used].reshape(DEC_BATCH, n_pages).astype(jnp.int32)
    return {
        'x_prompt': nrm(ks[0], (BATCH, SEQ, D_MODEL), 1.0),
        'x_sample': nrm(ks[1], (DEC_BATCH, DEC_SEQ, D_MODEL), 1.0),
        'cache_k': nrm(ks[2], (DEPTH, n_pool, PAGE_SIZE, N_HEADS, HEAD_DIM), 1.0),
        'cache_v': nrm(ks[3], (DEPTH, n_pool, PAGE_SIZE, N_HEADS, HEAD_DIM), 1.0),
        'state_h': nrm(ks[4], (DEPTH, DEC_BATCH, LRU_W), 0.5),
        'state_conv': nrm(ks[5], (DEPTH, DEC_BATCH, CONV_W - 1, LRU_W), 0.5),
        'page_table': page_table,
        'norm1_g': gain(ks[7], D_MODEL),
        'w_in': nrm(ks[8], (DEPTH, D_MODEL, IN_COLS), D_MODEL ** -0.5),
        'q_norm_g': gain(ks[9], HEAD_DIM),
        'k_norm_g': gain(ks[10], HEAD_DIM),
        'conv_w': nrm(ks[11], (DEPTH, CONV_W, LRU_W), CONV_W ** -0.5),
        'conv_b': nrm(ks[12], (DEPTH, LRU_W), 0.01),
        'w_rg': nrm(ks[13], (DEPTH, LRU_BLOCKS, LRU_BW, LRU_BW), LRU_BW ** -0.5),
        'b_rg': nrm(ks[14], (DEPTH, LRU_W), 0.01),
        'w_ig': nrm(ks[15], (DEPTH, LRU_BLOCKS, LRU_BW, LRU_BW), LRU_BW ** -0.5),
        'b_ig': nrm(ks[16], (DEPTH, LRU_W), 0.01),
        'lru_lambda': lam,
        'w_attn_o': nrm(ks[18], (DEPTH, ATTN_W, D_MODEL), ATTN_W ** -0.5),
        'w_lru_o': nrm(ks[19], (DEPTH, LRU_W, D_MODEL), LRU_W ** -0.5),
        'w_out': nrm(ks[20], (DEPTH, D_MODEL, D_MODEL), D_MODEL ** -0.5),
        'norm2_g': gain(ks[21], D_MODEL),
        'w_router': nrm(ks[22], (DEPTH, D_MODEL, N_EXPERTS), D_MODEL ** -0.5),
        'b_router': nrm(ks[23], (DEPTH, N_EXPERTS), 0.01),
        'w_gu': nrm(ks[24], (DEPTH, N_EXPERTS, D_MODEL, 2 * D_EXPERT), D_MODEL ** -0.5),
        'b_gu': nrm(ks[25], (DEPTH, N_EXPERTS, 2 * D_EXPERT), 0.01),
        'w_dn': nrm(ks[26], (DEPTH, N_EXPERTS, D_EXPERT, D_MODEL), D_EXPERT ** -0.5),
        'b_dn': nrm(ks[27], (DEPTH, N_EXPERTS, D_MODEL), 0.01),
    }


def reference(x_prompt, x_sample, cache_k, cache_v, state_h, state_conv, page_table, norm1_g, w_in,
              q_norm_g, k_norm_g, conv_w, conv_b, w_rg, b_rg, w_ig, b_ig, lru_lambda, w_attn_o,
              w_lru_o, w_out, norm2_g, w_router, b_router, w_gu, b_gu, w_dn, b_dn):
    n_dec, n_pages = page_table.shape
    past_len = n_pages * cache_k.shape[2]
    bp = x_prompt.shape[0]
    layer_weights = (norm1_g, w_in, q_norm_g, k_norm_g, conv_w, conv_b, w_rg, b_rg, w_ig, b_ig,
                     lru_lambda, w_attn_o, w_lru_o, w_out, norm2_g, w_router, b_router, w_gu, b_gu,
                     w_dn, b_dn)
    yp, ys = x_prompt, x_sample
    kp_l, vp_l, hp_l, cp_l, ks_l, vs_l, hs_l, cs_l = [], [], [], [], [], [], [], []
    for layer in range(DEPTH):
        w = [p[layer] for p in layer_weights]
        empty = jnp.zeros((bp, 0, N_HEADS, HEAD_DIM), x_prompt.dtype)
        yp, kp, vp, hp, cp = hybrid_layer(
            yp, 0, empty, empty,
            jnp.zeros((bp, CONV_W - 1, LRU_W), x_prompt.dtype), jnp.zeros((bp, LRU_W), x_prompt.dtype), *w)
        k_past = cache_k[layer, page_table].reshape(n_dec, past_len, N_HEADS, HEAD_DIM)
        v_past = cache_v[layer, page_table].reshape(n_dec, past_len, N_HEADS, HEAD_DIM)
        ys, kd, vd, hd, cd = hybrid_layer(ys, past_len, k_past, v_past, state_conv[layer], state_h[layer], *w)
        kp_l.append(kp); vp_l.append(vp); hp_l.append(hp); cp_l.append(cp)
        ks_l.append(kd); vs_l.append(vd); hs_l.append(hd); cs_l.append(cd)
    k_prompt = jnp.stack(kp_l)
    v_prompt = jnp.stack(vp_l)
    h_prompt = jnp.stack(hp_l)
    conv_prompt = jnp.stack(cp_l)
    k_sample = jnp.stack(ks_l)
    v_sample = jnp.stack(vs_l)
    h_sample = jnp.stack(hs_l)
    conv_sample = jnp.stack(cs_l)
    return (yp, ys, k_prompt, v_prompt, h_prompt, conv_prompt, k_sample, v_sample, h_sample, conv_sample)
```

```python
import functools

import jax
import jax.numpy as jnp
from jax import lax
from jax.experimental import pallas as pl
from jax.experimental.pallas import tpu as pltpu

F32 = jnp.float32
BF16 = jnp.bfloat16
I32 = jnp.int32

MOBA_BLOCK = 256
MOBA_TOPK = 3
ROPE_THETA = 10000.0
LRU_C = 8.0
TOP_K = 4
SWIGLU_LIMIT = 7.0
SWIGLU_ALPHA = 1.702
NORM_EPS = 1e-6

LANES = 128
SUBLANES = 8
VMEM_LIMIT = 56 * 1024 * 1024

NEG = float(jnp.finfo(jnp.float32).min)
HIGHEST = lax.Precision.HIGHEST
NT = (((1,), (1,)), ((), ()))


def _cparams(sem):
    return pltpu.CompilerParams(dimension_semantics=sem, vmem_limit_bytes=VMEM_LIMIT)


def _rope_table_kernel(inv_ref, cos_ref, sin_ref, *, pos0, rows, head_dim):
    i = pl.program_id(0)
    pos = lax.broadcasted_iota(I32, (rows, LANES), 0) + (pos0 + i * rows)
    lane = lax.broadcasted_iota(I32, (rows, LANES), 1)
    ang = pos.astype(F32) * inv_ref[...]
    first_half = (lane % head_dim) < (head_dim // 2)
    cos_ref[...] = jnp.cos(ang)
    s = jnp.sin(ang)
    sin_ref[...] = jnp.where(first_half, -s, s)


def _rope_table(n_pos, pos0, head_dim):
    half = head_dim // 2
    inv = ROPE_THETA ** (-jnp.arange(half, dtype=F32) * 2.0 / head_dim)
    inv_l = jnp.tile(inv, LANES // half).reshape(1, LANES)
    n_pad = -(-n_pos // SUBLANES) * SUBLANES
    rows = min(n_pad, 512)
    assert n_pad % rows == 0
    cos, sin = pl.pallas_call(
        functools.partial(_rope_table_kernel, pos0=pos0, rows=rows, head_dim=head_dim),
        grid=(n_pad // rows,),
        in_specs=[pl.BlockSpec((1, LANES), lambda i: (0, 0))],
        out_specs=[pl.BlockSpec((rows, LANES), lambda i: (i, 0))] * 2,
        out_shape=[jax.ShapeDtypeStruct((n_pad, LANES), F32)] * 2,
        name="rope_table",
    )(inv_l)
    return cos[:n_pos], sin[:n_pos]


def _inproj_kernel(x_ref, g1_ref, w_ref, qg_ref, kg_ref, cos_ref, sin_ref, bd_ref,
                   q_ref, k_ref, v_ref, kbf_ref, vt_ref, km_ref, xb_ref, gy_ref, sga_ref, sgl_ref,
                   *, aw, lw, d, head_dim, attn_aux):
    x = x_ref[...]
    xn = x * lax.rsqrt(jnp.mean(x * x, axis=-1, keepdims=True) + NORM_EPS) * g1_ref[...]
    xn = xn.astype(BF16)
    tm = x.shape[0]

    def proj(lo, width):
        return jnp.dot(xn, w_ref[:, lo:lo + width], preferred_element_type=F32)

    reps = aw // LANES
    cos = jnp.concatenate([cos_ref[...]] * reps, axis=1)
    sin = jnp.concatenate([sin_ref[...]] * reps, axis=1)
    lane = lax.broadcasted_iota(I32, (tm, aw), 1)
    first_half = (lane % head_dim) < (head_dim // 2)
    bd = bd_ref[...]

    def head_norm_rope(t, g):
        ss = t * t
        hi = ss.astype(BF16)
        lo = (ss - hi.astype(F32)).astype(BF16)
        ms = (jnp.dot(hi, bd, preferred_element_type=F32)
              + jnp.dot(lo, bd, preferred_element_type=F32))
        tn = t * lax.rsqrt(ms + NORM_EPS) * g
        up = pltpu.roll(tn, aw - head_dim // 2, 1)
        dn = pltpu.roll(tn, head_dim // 2, 1)
        return tn * cos + jnp.where(first_half, up, dn) * sin

    q_ref[...] = head_norm_rope(proj(0, aw), qg_ref[...])
    k = head_norm_rope(proj(aw, aw), kg_ref[...])
    k_ref[...] = k
    v = proj(2 * aw, aw)
    v_ref[...] = v
    if attn_aux:
        kbf_ref[...] = k.astype(BF16)
        km_ref[0] = jnp.mean(k, axis=0, keepdims=True)
        vt_ref[0] = v.T.astype(BF16)
    else:
        kbf_ref[...] = jnp.zeros(kbf_ref.shape, BF16)
        km_ref[...] = jnp.zeros(km_ref.shape, F32)
        vt_ref[...] = jnp.zeros(vt_ref.shape, BF16)
    xb_ref[...] = proj(3 * aw, lw)
    gy_ref[...] = jax.nn.gelu(proj(3 * aw + lw, lw))
    sga_ref[...] = jax.nn.sigmoid(proj(3 * aw + 2 * lw, d))
    sgl_ref[...] = jax.nn.sigmoid(proj(3 * aw + 2 * lw + d, d))


def _inproj(x2d, g1, w_bf, qg, kg, cos, sin, bd, *, tm, n_pos_tiles, aw, lw, head_dim, attn_aux):
    t, d = x2d.shape
    nt = t // tm
    row = lambda i: (i, 0)
    const = lambda i: (0, 0)
    pos = lambda i: (i % n_pos_tiles, 0)
    outs = pl.pallas_call(
        functools.partial(_inproj_kernel, aw=aw, lw=lw, d=d, head_dim=head_dim, attn_aux=attn_aux),
        grid=(nt,),
        in_specs=[
            pl.BlockSpec((tm, d), row),
            pl.BlockSpec((1, d), const),
            pl.BlockSpec(w_bf.shape, const),
            pl.BlockSpec((1, aw), const),
            pl.BlockSpec((1, aw), const),
            pl.BlockSpec((tm, LANES), pos),
            pl.BlockSpec((tm, LANES), pos),
            pl.BlockSpec((aw, aw), const),
        ],
        out_specs=[
            pl.BlockSpec((tm, aw), row),
            pl.BlockSpec((tm, aw), row),
            pl.BlockSpec((tm, aw), row),
            pl.BlockSpec((tm, aw), row),
            pl.BlockSpec((1, aw, tm), lambda i: (i, 0, 0)),
            pl.BlockSpec((1, 1, aw), lambda i: (i, 0, 0)),
            pl.BlockSpec((tm, lw), row),
            pl.BlockSpec((tm, lw), row),
            pl.BlockSpec((tm, d), row),
            pl.BlockSpec((tm, d), row),
        ],
        out_shape=[
            jax.ShapeDtypeStruct((t, aw), F32),
            jax.ShapeDtypeStruct((t, aw), F32),
            jax.ShapeDtypeStruct((t, aw), F32),
            jax.ShapeDtypeStruct((t, aw), BF16),
            jax.ShapeDtypeStruct((nt, aw, tm), BF16),
            jax.ShapeDtypeStruct((nt, 1, aw), F32),
            jax.ShapeDtypeStruct((t, lw), F32),
            jax.ShapeDtypeStruct((t, lw), F32),
            jax.ShapeDtypeStruct((t, d), F32),
            jax.ShapeDtypeStruct((t, d), F32),
        ],
        compiler_params=_cparams(("arbitrary",)),
        name="inproj",
    )(x2d, g1, w_bf, qg, kg, cos, sin, bd)
    return outs


def _top_rows(g, n_rows, topk, valid_floor):
    row = lax.broadcasted_iota(I32, g.shape, 0)
    sel = jnp.zeros(g.shape, dtype=jnp.bool_)
    idxs = []
    vals = []
    for _ in range(topk):
        mx = jnp.max(g, axis=0, keepdims=True)
        idx = jnp.min(jnp.where(g == mx, row, n_rows), axis=0, keepdims=True)
        hit = row == idx
        sel = sel | (hit & (mx > valid_floor))
        g = jnp.where(hit, -jnp.inf, g)
        idxs.append(idx)
        vals.append(mx)
    return idxs, vals, sel


def _attn_kernel(q_ref, k_ref, vt_ref, km_ref, o_ref, sel_ref, *, nb, blk, head_dim, topk):
    i = pl.program_id(2)
    q = q_ref[0]
    lane = lax.broadcasted_iota(I32, (blk, LANES), 1)
    krow = lax.broadcasted_iota(I32, (blk, blk), 0)
    qcol = lax.broadcasted_iota(I32, (blk, blk), 1)
    brow = lax.broadcasted_iota(I32, (nb, blk), 0)
    scale = head_dim ** -0.5
    outs = []
    for hh in range(LANES // head_dim):
        in_head = (lane >= hh * head_dim) & (lane < (hh + 1) * head_dim)
        qm = jnp.where(in_head, q, 0.0)
        gate = lax.dot_general(km_ref[0], qm, NT, precision=HIGHEST, preferred_element_type=F32)
        gate = jnp.where(brow < i, gate, NEG)
        _, _, sel = _top_rows(gate, nb, topk, NEG)
        sel_ref[...] = sel.astype(F32)
        qs = (qm * scale).astype(BF16)
        hs = slice(hh * head_dim, (hh + 1) * head_dim)

        s = lax.dot_general(k_ref[0, i], qs, NT, preferred_element_type=F32)
        s = jnp.where(krow <= qcol, s, NEG)
        m0 = jnp.max(s, axis=0, keepdims=True)
        p = jnp.exp(s - m0)
        l0 = jnp.sum(p, axis=0, keepdims=True)
        acc0 = jnp.dot(vt_ref[0, i, hs, :], p.astype(BF16), preferred_element_type=F32)

        def body(j, carry):
            m, l, acc = carry
            sj = lax.dot_general(k_ref[0, j], qs, NT, preferred_element_type=F32)
            sj = jnp.where(sel_ref[pl.ds(j, 1), :] > 0.5, sj, NEG)
            m_new = jnp.maximum(m, jnp.max(sj, axis=0, keepdims=True))
            alpha = jnp.exp(m - m_new)
            pj = jnp.exp(sj - m_new)
            l = alpha * l + jnp.sum(pj, axis=0, keepdims=True)
            acc = alpha * acc + jnp.dot(vt_ref[0, j, hs, :], pj.astype(BF16),
                                        preferred_element_type=F32)
            return m_new, l, acc

        _, l, acc = lax.fori_loop(0, i, body, (m0, l0, acc0))
        outs.append(acc / l)
    o_ref[0] = jnp.concatenate(outs, axis=0).T.astype(BF16)


def _attention(q3, k4, vt4, km3, *, head_dim, topk):
    b, s, aw = q3.shape
    nb, blk = k4.shape[1], k4.shape[2]
    return pl.pallas_call(
        functools.partial(_attn_kernel, nb=nb, blk=blk, head_dim=head_dim, topk=topk),
        grid=(b, aw // LANES, nb),
        in_specs=[
            pl.BlockSpec((1, blk, LANES), lambda bi, hp, i: (bi, i, hp)),
            pl.BlockSpec((1, nb, blk, LANES), lambda bi, hp, i: (bi, 0, 0, hp)),
            pl.BlockSpec((1, nb, LANES, blk), lambda bi, hp, i: (bi, 0, hp, 0)),
            pl.BlockSpec((1, nb, LANES), lambda bi, hp, i: (bi, 0, hp)),
        ],
        out_specs=pl.BlockSpec((1, blk, LANES), lambda bi, hp, i: (bi, i, hp)),
        out_shape=jax.ShapeDtypeStruct((b, s, aw), BF16),
        scratch_shapes=[pltpu.VMEM((nb, blk), F32)],
        compiler_params=_cparams(("arbitrary", "arbitrary", "arbitrary")),
        name="moba_attn",
    )(q3, k4, vt4, km3)


def _dec_gate_kernel(pt_ref, q_ref, *refs, n_pp, ppb, n_groups, blk, topk, nbp):
    page_refs = refs[:n_pp]
    hs_ref, sel_ref, km_sc = refs[n_pp:]
    g = pl.program_id(1)
    per_step = n_pp // ppb
    for u in range(per_step):
        tot = page_refs[u * ppb][0].sum(axis=0, keepdims=True)
        for w in range(1, ppb):
            tot = tot + page_refs[u * ppb + w][0].sum(axis=0, keepdims=True)
        km_sc[pl.ds(g * per_step + u, 1), :] = tot * (1.0 / blk)

    @pl.when(g == n_groups - 1)
    def _():
        prod = km_sc[...] * q_ref[0]
        gate = jnp.dot(prod, hs_ref[...], precision=HIGHEST, preferred_element_type=F32)
        idxs, _, _ = _top_rows(gate, nbp, topk, NEG)
        pad = jnp.zeros((SUBLANES - topk, LANES), I32)
        sel_ref[0] = jnp.concatenate(idxs + [pad], axis=0)


def _dec_select(page_table, q3, cache_k3, head_sum, *, ppb, blk, topk):
    nbd, n_pages = page_table.shape
    page, aw = cache_k3.shape[1], cache_k3.shape[2]
    n_pp = ppb
    while n_pp * 2 <= 16 and n_pages % (n_pp * 2) == 0:
        n_pp *= 2
    n_groups = n_pages // n_pp
    nbp = n_pages // ppb

    def page_spec(u):
        return pl.BlockSpec((1, page, aw), lambda bi, g, pt: (pt[bi, g * n_pp + u], 0, 0))

    grid_spec = pltpu.PrefetchScalarGridSpec(
        num_scalar_prefetch=1,
        grid=(nbd, n_groups),
        in_specs=[pl.BlockSpec((1, 1, aw), lambda bi, g, pt: (bi, 0, 0))]
        + [page_spec(u) for u in range(n_pp)]
        + [pl.BlockSpec((aw, LANES), lambda bi, g, pt: (0, 0))],
        out_specs=pl.BlockSpec((1, SUBLANES, LANES), lambda bi, g, pt: (bi, 0, 0)),
        scratch_shapes=[pltpu.VMEM((nbp, aw), F32)],
    )
    return pl.pallas_call(
        functools.partial(_dec_gate_kernel, n_pp=n_pp, ppb=ppb, n_groups=n_groups, blk=blk,
                          topk=topk, nbp=nbp),
        grid_spec=grid_spec,
        out_shape=jax.ShapeDtypeStruct((nbd, SUBLANES, LANES), I32),
        compiler_params=_cparams(("arbitrary", "arbitrary")),
        name="dec_block_select",
    )(page_table, q3, *([cache_k3] * n_pp), head_sum)


def _dec_attn_kernel(pt_ref, sel_ref, q_ref, kn_ref, vn_ref, bd_ref, *refs,
                     n_heads, head_dim, ppb, topk):
    n_pg = n_heads * ppb
    k_refs = refs[:n_pg]
    v_refs = refs[n_pg:2 * n_pg]
    o_ref, m_sc, l_sc, acc_sc = refs[2 * n_pg:]
    r = pl.program_id(1)
    q = q_ref[0]
    aw = q.shape[1]
    scale = head_dim ** -0.5
    lane = lax.broadcasted_iota(I32, (1, aw), 1)

    @pl.when(r == 0)
    def _():
        own = jnp.broadcast_to(q * kn_ref[0], (SUBLANES, aw))
        s_own = jnp.dot(own, bd_ref[...], precision=HIGHEST, preferred_element_type=F32)
        m_sc[...] = s_own[0:1] * scale
        l_sc[...] = jnp.ones((1, aw), F32)
        acc_sc[...] = vn_ref[0]

    for h in range(n_heads):
        in_head = (lane >= h * head_dim) & (lane < (h + 1) * head_dim)
        kh = jnp.concatenate([k_refs[h * ppb + w][0] for w in range(ppb)], axis=0).astype(BF16)
        vh = jnp.concatenate([v_refs[h * ppb + w][0] for w in range(ppb)], axis=0).astype(BF16)
        qm = jnp.broadcast_to(jnp.where(in_head, q, 0.0) * scale, (SUBLANES, aw)).astype(BF16)
        s = lax.dot_general(qm, kh, NT, preferred_element_type=F32)[0:1]
        m_old = m_sc[...]
        l_old = l_sc[...]
        acc = acc_sc[...]
        m_h = jnp.max(jnp.where(in_head, m_old, -jnp.inf), axis=1, keepdims=True)
        m_new_h = jnp.maximum(m_h, jnp.max(s, axis=1, keepdims=True))
        p = jnp.exp(s - m_new_h)
        p_sum = jnp.sum(p, axis=1, keepdims=True)
        pv = jnp.dot(jnp.broadcast_to(p, (SUBLANES, p.shape[1])).astype(BF16), vh,
                     preferred_element_type=F32)[0:1]
        alpha = jnp.exp(m_old - m_new_h)
        m_sc[...] = jnp.where(in_head, m_new_h, m_old)
        l_sc[...] = jnp.where(in_head, alpha * l_old + p_sum, l_old)
        acc_sc[...] = jnp.where(in_head, alpha * acc + pv, acc)

    @pl.when(r == topk - 1)
    def _():
        o_ref[0] = acc_sc[...] / l_sc[...]


def _dec_attention(page_table, sel, q3, kn3, vn3, bd_f32, cache_k3, cache_v3, *,
                   n_heads, head_dim, ppb, topk):
    nbd = page_table.shape[0]
    page, aw = cache_k3.shape[1], cache_k3.shape[2]

    def page_spec(h, w):
        return pl.BlockSpec(
            (1, page, aw),
            lambda bi, r, pt, sl: (pt[bi, ppb * sl[bi, h, r] + w], 0, 0))

    vec = pl.BlockSpec((1, 1, aw), lambda bi, r, pt, sl: (bi, 0, 0))
    pages = [page_spec(h, w) for h in range(n_heads) for w in range(ppb)]
    grid_spec = pltpu.PrefetchScalarGridSpec(
        num_scalar_prefetch=2,
        grid=(nbd, topk),
        in_specs=[vec, vec, vec, pl.BlockSpec((aw, aw), lambda bi, r, pt, sl: (0, 0))]
        + pages + pages,
        out_specs=vec,
        scratch_shapes=[pltpu.VMEM((1, aw), F32)] * 3,
    )
    n_pg = n_heads * ppb
    return pl.pallas_call(
        functools.partial(_dec_attn_kernel, n_heads=n_heads, head_dim=head_dim, ppb=ppb, topk=topk),
        grid_spec=grid_spec,
        out_shape=jax.ShapeDtypeStruct((nbd, 1, aw), F32),
        compiler_params=_cparams(("arbitrary", "arbitrary")),
        name="dec_attn",
    )(page_table, sel, q3, kn3, vn3, bd_f32, *([cache_k3] * n_pg), *([cache_v3] * n_pg))


def _lru_gates(xc, wrg_ref, brg, wig_ref, big, lam, first_pos_mask):
    nblk, bw = wrg_ref.shape[0], wrg_ref.shape[1]
    xcb = xc.astype(BF16)
    rz = jnp.concatenate(
        [jnp.dot(xcb[:, n * bw:(n + 1) * bw], wrg_ref[n], preferred_element_type=F32)
         for n in range(nblk)], axis=1)
    iz = jnp.concatenate(
        [jnp.dot(xcb[:, n * bw:(n + 1) * bw], wig_ref[n], preferred_element_type=F32)
         for n in range(nblk)], axis=1)
    r = jax.nn.sigmoid(rz + brg)
    ig = jax.nn.sigmoid(iz + big)
    neg_lam = -lam
    softplus = jnp.maximum(neg_lam, 0.0) + jnp.log1p(jnp.exp(-jnp.abs(neg_lam)))
    log_a = -LRU_C * r * softplus
    a = jnp.exp(log_a)
    mult = jnp.sqrt(jnp.tanh(-log_a) * (a * a + 1.0))
    if first_pos_mask is not None:
        mult = jnp.where(first_pos_mask, 1.0, mult)
    return a, xc * ig * mult


def _lru_kernel(xb_ref, gy_ref, cs_ref, h0_ref, cw_ref, cb_ref, wrg_ref, brg_ref, wig_ref, big_ref,
                lam_ref, out_ref, hlast_ref, xext, a_sc, u_sc, hcar, *, ts, chunk, pos0):
    t = pl.program_id(1)
    w = xb_ref.shape[1]
    halo = SUBLANES
    n_tap = cw_ref.shape[0]

    @pl.when(t == 0)
    def _():
        xext[0:halo, :] = jnp.zeros((halo, w), F32)
        xext[halo - (n_tap - 1):halo, :] = cs_ref[0]
        hcar[...] = h0_ref[0]

    xext[halo:halo + ts, :] = xb_ref[...]
    xc = cb_ref[...]
    for j in range(n_tap):
        lo = halo - (n_tap - 1) + j
        xc = xc + xext[lo:lo + ts, :] * cw_ref[j:j + 1, :]
    xext[0:halo, :] = xext[ts:ts + halo, :]

    row = lax.broadcasted_iota(I32, (ts, w), 0)
    first = (row + t * ts + pos0) == 0
    a, u = _lru_gates(xc, wrg_ref, brg_ref[...], wig_ref, big_ref[...], lam_ref[...], first)
    n_lc = w // LANES
    for lc in range(n_lc):
        a_sc[lc] = a[:, lc * LANES:(lc + 1) * LANES]
        u_sc[lc] = u[:, lc * LANES:(lc + 1) * LANES]

    hs = [jnp.zeros((SUBLANES, LANES), F32)] * n_lc
    prods = [jnp.ones((SUBLANES, LANES), F32)] * n_lc
    for l in range(chunk):
        idx = pl.ds(l, SUBLANES, stride=chunk)
        for lc in range(n_lc):
            a8 = a_sc[lc, idx, :]
            hs[lc] = a8 * hs[lc] + u_sc[lc, idx, :]
            prods[lc] = a8 * prods[lc]
            u_sc[lc, idx, :] = hs[lc]
            a_sc[lc, idx, :] = prods[lc]

    carry = hcar[...]
    for c in range(SUBLANES):
        rows = slice(c * chunk, (c + 1) * chunk)
        ends = []
        for lc in range(n_lc):
            cols = slice(lc * LANES, (lc + 1) * LANES)
            h_true = u_sc[lc, rows, :] + a_sc[lc, rows, :] * carry[:, cols]
            out_ref[rows, cols] = (h_true * gy_ref[rows, cols]).astype(BF16)
            ends.append(h_true[chunk - 1:chunk, :])
        carry = jnp.concatenate(ends, axis=1)
    hcar[...] = carry
    hlast_ref[0] = carry


def _lru_prompt(xb, gy, cs, h0, cw, cb, wrg, brg, wig, big, lam, *, b, s, ts, pos0):
    w = xb.shape[1]
    nt = s // ts
    chunk = ts // SUBLANES
    n_tap = cw.shape[0]
    row = lambda bi, t: (bi * nt + t, 0)
    const2 = lambda bi, t: (0, 0)
    const3 = lambda bi, t: (0, 0, 0)
    per_b = lambda bi, t: (bi, 0, 0)
    return pl.pallas_call(
        functools.partial(_lru_kernel, ts=ts, chunk=chunk, pos0=pos0),
        grid=(b, nt),
        in_specs=[
            pl.BlockSpec((ts, w), row),
            pl.BlockSpec((ts, w), row),
            pl.BlockSpec((1, n_tap - 1, w), per_b),
            pl.BlockSpec((1, 1, w), per_b),
            pl.BlockSpec(cw.shape, const2),
            pl.BlockSpec((1, w), const2),
            pl.BlockSpec(wrg.shape, const3),
            pl.BlockSpec((1, w), const2),
            pl.BlockSpec(wig.shape, const3),
            pl.BlockSpec((1, w), const2),
            pl.BlockSpec((1, w), const2),
        ],
        out_specs=[pl.BlockSpec((ts, w), row), pl.BlockSpec((1, 1, w), per_b)],
        out_shape=[jax.ShapeDtypeStruct((b * s, w), BF16), jax.ShapeDtypeStruct((b, 1, w), F32)],
        scratch_shapes=[
            pltpu.VMEM((ts + SUBLANES, w), F32),
            pltpu.VMEM((w // LANES, ts, LANES), F32),
            pltpu.VMEM((w // LANES, ts, LANES), F32),
            pltpu.VMEM((1, w), F32),
        ],
        compiler_params=_cparams(("arbitrary", "arbitrary")),
        name="rglru_scan",
    )(xb, gy, cs, h0, cw, cb, wrg, brg, wig, big, lam)


def _lru_step_kernel(xb_ref, gy_ref, cs_ref, h0_ref, cw_ref, cb_ref, wrg_ref, brg_ref, wig_ref,
                     big_ref, lam_ref, out_ref, h_ref):
    n_tap = cw_ref.shape[0]
    xc = cb_ref[...]
    for j in range(n_tap - 1):
        xc = xc + cs_ref[j] * cw_ref[j:j + 1, :]
    xc = xc + xb_ref[...] * cw_ref[n_tap - 1:n_tap, :]
    a, u = _lru_gates(xc, wrg_ref, brg_ref[...], wig_ref, big_ref[...], lam_ref[...], None)
    h = a * h0_ref[...] + u
    h_ref[...] = h
    out_ref[...] = h * gy_ref[...]


def _lru_step(xb, gy, cs_t, h0, cw, cb, wrg, brg, wig, big, lam):
    n, w = xb.shape
    return pl.pallas_call(
        _lru_step_kernel,
        out_shape=[jax.ShapeDtypeStruct((n, w), F32), jax.ShapeDtypeStruct((n, w), F32)],
        name="rglru_step",
    )(xb, gy, cs_t, h0, cw, cb, wrg, brg, wig, big, lam)


def _merge_kernel(attn_ref, lru_ref, sga_ref, sgl_ref, x_ref, wao_ref, wlo_ref, wout_ref, g2_ref,
                  wrt_ref, br_ref, h_ref, hn_ref, te_ref, gw_ref, rk_ref, cnt_ref, carry_sc,
                  *, n_exp, topk):
    i = pl.program_id(0)
    tm = x_ref.shape[0]

    @pl.when(i == 0)
    def _():
        carry_sc[...] = jnp.zeros(carry_sc.shape, F32)

    att = jnp.dot(attn_ref[...].astype(BF16), wao_ref[...], preferred_element_type=F32)
    lru = jnp.dot(lru_ref[...].astype(BF16), wlo_ref[...], preferred_element_type=F32)
    mixed = sga_ref[...] * att + sgl_ref[...] * lru
    h = x_ref[...] + jnp.dot(mixed.astype(BF16), wout_ref[...], preferred_element_type=F32)
    h_ref[...] = h
    hn = h * lax.rsqrt(jnp.mean(h * h, axis=-1, keepdims=True) + NORM_EPS) * g2_ref[...]
    hn_ref[...] = hn

    logits = lax.dot_general(wrt_ref[...], hn, NT, precision=HIGHEST,
                             preferred_element_type=F32) + br_ref[...]
    idxs, vals, sel = _top_rows(logits, n_exp, topk, -jnp.inf)
    top_v = jnp.concatenate(vals, axis=0)
    ex = jnp.exp(top_v - top_v[0:1])
    gw_ref[...] = ex / jnp.sum(ex, axis=0, keepdims=True)
    te_ref[...] = jnp.concatenate(idxs, axis=0)

    before = lax.broadcasted_iota(I32, (tm, tm), 0) < lax.broadcasted_iota(I32, (tm, tm), 1)
    sel_bf = sel.astype(BF16)
    prior = jnp.dot(sel_bf, before.astype(BF16), preferred_element_type=F32)
    base = prior + carry_sc[:, 0:1]
    erow = lax.broadcasted_iota(I32, (n_exp, tm), 0)
    ranks = [jnp.sum(jnp.where(erow == idx, base, 0.0), axis=0, keepdims=True) for idx in idxs]
    rk_ref[...] = jnp.concatenate(ranks, axis=0).astype(I32)
    carry_sc[...] = carry_sc[...] + jnp.sum(sel.astype(F32), axis=1, keepdims=True)
    cnt_ref[...] = carry_sc[...]


def _merge(attn, lru, sga, sgl, x2d, wao, wlo, wout, g2, wrt, br, *, tm, n_exp, topk):
    t, d = x2d.shape
    aw, lw = attn.shape[1], lru.shape[1]
    row = lambda i: (i, 0)
    col = lambda i: (0, i)
    const = lambda i: (0, 0)
    return pl.pallas_call(
        functools.partial(_merge_kernel, n_exp=n_exp, topk=topk),
        grid=(t // tm,),
        in_specs=[
            pl.BlockSpec((tm, aw), row),
            pl.BlockSpec((tm, lw), row),
            pl.BlockSpec((tm, d), row),
            pl.BlockSpec((tm, d), row),
            pl.BlockSpec((tm, d), row),
            pl.BlockSpec((aw, d), const),
            pl.BlockSpec((lw, d), const),
            pl.BlockSpec((d, d), const),
            pl.BlockSpec((1, d), const),
            pl.BlockSpec((n_exp, d), const),
            pl.BlockSpec((n_exp, 1), const),
        ],
        out_specs=[
            pl.BlockSpec((tm, d), row),
            pl.BlockSpec((tm, d), row),
            pl.BlockSpec((topk, tm), col),
            pl.BlockSpec((topk, tm), col),
            pl.BlockSpec((topk, tm), col),
            pl.BlockSpec((n_exp, LANES), const),
        ],
        out_shape=[
            jax.ShapeDtypeStruct((t, d), F32),
            jax.ShapeDtypeStruct((t, d), F32),
            jax.ShapeDtypeStruct((topk, t), I32),
            jax.ShapeDtypeStruct((topk, t), F32),
            jax.ShapeDtypeStruct((topk, t), I32),
            jax.ShapeDtypeStruct((n_exp, LANES), F32),
        ],
        scratch_shapes=[pltpu.VMEM((n_exp, LANES), F32)],
        compiler_params=_cparams(("arbitrary",)),
        name="merge_router",
    )(attn, lru, sga, sgl, x2d, wao, wlo, wout, g2, wrt, br)


def _row_copy(src, src_row, dst, dst_row, sem):
    return pltpu.make_async_copy(src.at[pl.ds(src_row, 1)], dst.at[pl.ds(dst_row, 1)], sem)


def _dispatch_kernel(dest_ref, hn_ref, buf_in, buf_out, sem, *, topk):
    del buf_in
    tm = hn_ref.shape[0]

    def issue(t, c):
        for k in range(topk):
            _row_copy(hn_ref, t, buf_out, dest_ref[k, t], sem).start()
        return c

    lax.fori_loop(0, tm, issue, 0)

    def drain(t, c):
        for k in range(topk):
            _row_copy(hn_ref, 0, buf_out, 0, sem).wait()
        return c

    lax.fori_loop(0, tm, drain, 0)


def _dispatch(dest, hn, n_rows, *, tm, topk):
    t, d = hn.shape
    buf0 = jnp.zeros((n_rows, d), hn.dtype)
    return pl.pallas_call(
        functools.partial(_dispatch_kernel, topk=topk),
        grid=(t // tm,),
        in_specs=[
            pl.BlockSpec((topk, tm), lambda i: (0, i), memory_space=pltpu.SMEM),
            pl.BlockSpec((tm, d), lambda i: (i, 0)),
            pl.BlockSpec(memory_space=pl.ANY),
        ],
        out_specs=pl.BlockSpec(memory_space=pl.ANY),
        out_shape=jax.ShapeDtypeStruct((n_rows, d), hn.dtype),
        scratch_shapes=[pltpu.SemaphoreType.DMA(())],
        input_output_aliases={2: 0},
        compiler_params=_cparams(("arbitrary",)),
        name="moe_dispatch",
    )(dest, hn, buf0)


def _expert_kernel(be_ref, nu_ref, x_ref, wg_ref, wl_ref, wd_ref, bg_ref, bl_ref, bd_ref, o_ref):
    i = pl.program_id(0)

    @pl.when(i < nu_ref[0])
    def _():
        x = x_ref[...].astype(BF16)
        g = jnp.dot(x, wg_ref[0], preferred_element_type=F32) + bg_ref[0]
        lin = jnp.dot(x, wl_ref[0], preferred_element_type=F32) + bl_ref[0]
        glu = jnp.minimum(g, SWIGLU_LIMIT)
        lin = jnp.clip(lin, -SWIGLU_LIMIT, SWIGLU_LIMIT)
        act = glu * jax.nn.sigmoid(SWIGLU_ALPHA * glu) * (lin + 1.0)
        o_ref[...] = jnp.dot(act.astype(BF16), wd_ref[0], preferred_element_type=F32) + bd_ref[0]

    @pl.when(i >= nu_ref[0])
    def _():
        o_ref[...] = jnp.zeros(o_ref.shape, F32)


def _experts(blk_e, n_used, buf, wg, wl, wd, bg, bl, bd, *, rows):
    n_rows, d = buf.shape
    de = wg.shape[2]
    xrow = lambda i, be, nu: (i, 0)
    wsel = lambda i, be, nu: (be[i], 0, 0)
    grid_spec = pltpu.PrefetchScalarGridSpec(
        num_scalar_prefetch=2,
        grid=(n_rows // rows,),
        in_specs=[
            pl.BlockSpec((rows, d), xrow),
            pl.BlockSpec((1, d, de), wsel),
            pl.BlockSpec((1, d, de), wsel),
            pl.BlockSpec((1, de, d), wsel),
            pl.BlockSpec((1, 1, de), wsel),
            pl.BlockSpec((1, 1, de), wsel),
            pl.BlockSpec((1, 1, d), wsel),
        ],
        out_specs=pl.BlockSpec((rows, d), xrow),
    )
    return pl.pallas_call(
        _expert_kernel,
        grid_spec=grid_spec,
        out_shape=jax.ShapeDtypeStruct((n_rows, d), F32),
        compiler_params=_cparams(("arbitrary",)),
        name="moe_experts",
    )(blk_e, n_used, buf, wg, wl, wd, bg, bl, bd)


def _combine_kernel(dest_ref, gw_ref, h_ref, obuf, y_ref, g_sc, sem, *, topk):
    tm = h_ref.shape[0]

    def issue(t, c):
        for k in range(topk):
            _row_copy(obuf, dest_ref[k, t], g_sc.at[k], t, sem).start()
        return c

    lax.fori_loop(0, tm, issue, 0)

    def drain(t, c):
        for k in range(topk):
            _row_copy(obuf, 0, g_sc.at[k], 0, sem).wait()
        return c

    lax.fori_loop(0, tm, drain, 0)
    gw = gw_ref[...]
    ff = g_sc[0] * gw[:, 0:1]
    for k in range(1, topk):
        ff = ff + g_sc[k] * gw[:, k:k + 1]
    y_ref[...] = h_ref[...] + ff


def _combine(dest, gw_t, h, obuf, *, tm, topk):
    t, d = h.shape
    return pl.pallas_call(
        functools.partial(_combine_kernel, topk=topk),
        grid=(t // tm,),
        in_specs=[
            pl.BlockSpec((topk, tm), lambda i: (0, i), memory_space=pltpu.SMEM),
            pl.BlockSpec((tm, topk), lambda i: (i, 0)),
            pl.BlockSpec((tm, d), lambda i: (i, 0)),
            pl.BlockSpec(memory_space=pl.ANY),
        ],
        out_specs=pl.BlockSpec((tm, d), lambda i: (i, 0)),
        out_shape=jax.ShapeDtypeStruct((t, d), F32),
        scratch_shapes=[pltpu.VMEM((topk, tm, d), F32), pltpu.SemaphoreType.DMA(())],
        compiler_params=_cparams(("arbitrary",)),
        name="moe_combine",
    )(dest, gw_t, h, obuf)


def _moe(h, hn, te, gw, rk, cnt, wts, *, rows, tm_io, topk):
    t, d = h.shape
    wg, wl, wd, bg, bl, bd = wts
    n_exp = wg.shape[0]
    counts = cnt[:, 0].astype(I32)
    padded = (counts + rows - 1) // rows * rows
    pad_end = jnp.cumsum(padded)
    pad_start = pad_end - padded
    dest = pad_start[te] + rk
    n_blk = -(-(t * topk) // rows) + n_exp
    blk_e = jnp.minimum(
        jnp.searchsorted(pad_end, jnp.arange(n_blk, dtype=I32) * rows, side="right"),
        n_exp - 1).astype(I32)
    n_used = (pad_end[-1:] // rows).astype(I32)
    buf = _dispatch(dest, hn, n_blk * rows, tm=tm_io, topk=topk)
    obuf = _experts(blk_e, n_used, buf, wg, wl, wd, bg, bl, bd, rows=rows)
    return _combine(dest, gw.T, h, obuf, tm=tm_io, topk=topk)


def kernel(x_prompt, x_sample, cache_k, cache_v, state_h, state_conv, page_table, norm1_g, w_in,
           q_norm_g, k_norm_g, conv_w, conv_b, w_rg, b_rg, w_ig, b_ig, lru_lambda, w_attn_o,
           w_lru_o, w_out, norm2_g, w_router, b_router, w_gu, b_gu, w_dn, b_dn):
    depth = w_in.shape[0]
    assert depth == 1, "one layer per step is supported"
    bp, sp, d = x_prompt.shape
    nbd, sd, _ = x_sample.shape
    assert sd == 1, "decode handles one new token per sequence"
    page, n_heads, head_dim = cache_k.shape[2], cache_k.shape[3], cache_k.shape[4]
    aw = n_heads * head_dim
    lw = w_lru_o.shape[1]
    n_exp = w_router.shape[2]
    n_pages = page_table.shape[1]
    past_len = n_pages * page
    blk = MOBA_BLOCK
    assert sp % blk == 0 and past_len % blk == 0 and blk % page == 0
    assert LANES % head_dim == 0 and aw % LANES == 0
    ppb = blk // page
    assert past_len // blk >= MOBA_TOPK

    w_in_bf = w_in[0].astype(BF16)
    wrg_bf, wig_bf = w_rg[0].astype(BF16), w_ig[0].astype(BF16)
    wao_bf, wlo_bf, wout_bf = w_attn_o[0].astype(BF16), w_lru_o[0].astype(BF16), w_out[0].astype(BF16)
    wg_bf = w_gu[0][:, :, 0::2].astype(BF16)
    wl_bf = w_gu[0][:, :, 1::2].astype(BF16)
    wd_bf = w_dn[0].astype(BF16)
    bg = b_gu[0][:, None, 0::2]
    bl = b_gu[0][:, None, 1::2]
    bdn = b_dn[0][:, None, :]
    moe_w = (wg_bf, wl_bf, wd_bf, bg, bl, bdn)
    qg = jnp.tile(q_norm_g[0], n_heads).reshape(1, aw)
    kg = jnp.tile(k_norm_g[0], n_heads).reshape(1, aw)
    head_of = jnp.arange(aw, dtype=I32) // head_dim
    same_head = head_of[:, None] == head_of[None, :]
    bd_mean = (same_head.astype(F32) / head_dim).astype(BF16)
    bd_sum = same_head.astype(F32)
    head_sum = (head_of[:, None] == jnp.arange(LANES, dtype=I32)[None, :]).astype(F32)
    g1 = norm1_g[0].reshape(1, d)
    g2 = norm2_g[0].reshape(1, d)
    wrt = w_router[0].T
    br = b_router[0].reshape(n_exp, 1)
    cw, cb = conv_w[0], conv_b[0].reshape(1, lw)
    brg, big, lam = b_rg[0].reshape(1, lw), b_ig[0].reshape(1, lw), lru_lambda[0].reshape(1, lw)
    n_tap = cw.shape[0]

    tp = bp * sp
    cos_p, sin_p = _rope_table(sp, 0, head_dim)
    xp2 = x_prompt.reshape(tp, d)
    (q_p, k_p, v_p, kbf_p, vt_p, km_p, xb_p, gy_p, sga_p, sgl_p) = _inproj(
        xp2, g1, w_in_bf, qg, kg, cos_p, sin_p, bd_mean,
        tm=blk, n_pos_tiles=sp // blk, aw=aw, lw=lw, head_dim=head_dim, attn_aux=True)
    nb = sp // blk
    attn_p = _attention(
        q_p.reshape(bp, sp, aw), kbf_p.reshape(bp, nb, blk, aw), vt_p.reshape(bp, nb, aw, blk),
        km_p.reshape(bp, nb, aw), head_dim=head_dim, topk=MOBA_TOPK).reshape(tp, aw)
    lru_p, hlast_p = _lru_prompt(
        xb_p, gy_p, jnp.zeros((bp, n_tap - 1, lw), F32), jnp.zeros((bp, 1, lw), F32),
        cw, cb, wrg_bf, brg, wig_bf, big, lam, b=bp, s=sp, ts=blk, pos0=0)
    h_p, hn_p, te_p, gw_p, rk_p, cnt_p = _merge(
        attn_p, lru_p, sga_p, sgl_p, xp2, wao_bf, wlo_bf, wout_bf, g2, wrt, br,
        tm=blk, n_exp=n_exp, topk=TOP_K)
    y_p = _moe(h_p, hn_p, te_p, gw_p, rk_p, cnt_p, moe_w, rows=256, tm_io=blk, topk=TOP_K)
    assert sp >= n_tap - 1
    conv_p = xb_p.reshape(bp, sp, lw)[:, sp - (n_tap - 1):]

    cos_d, sin_d = _rope_table(sd, past_len, head_dim)
    cos_d = jnp.tile(cos_d, (nbd, 1))
    sin_d = jnp.tile(sin_d, (nbd, 1))
    xd2 = x_sample.reshape(nbd, d)
    (q_d, k_d, v_d, _, _, _, xb_d, gy_d, sga_d, sgl_d) = _inproj(
        xd2, g1, w_in_bf, qg, kg, cos_d, sin_d, bd_mean,
        tm=nbd, n_pos_tiles=1, aw=aw, lw=lw, head_dim=head_dim, attn_aux=False)
    n_pool = cache_k.shape[1]
    ck3 = cache_k[0].reshape(n_pool, page, aw)
    cv3 = cache_v[0].reshape(n_pool, page, aw)
    q_d3 = q_d.reshape(nbd, 1, aw)
    sel = _dec_select(page_table, q_d3, ck3, head_sum, ppb=ppb, blk=blk, topk=MOBA_TOPK)
    sel = sel[:, :MOBA_TOPK, :n_heads].transpose(0, 2, 1)
    attn_d = _dec_attention(
        page_table, sel, q_d3, k_d.reshape(nbd, 1, aw), v_d.reshape(nbd, 1, aw), bd_sum, ck3, cv3,
        n_heads=n_heads, head_dim=head_dim, ppb=ppb, topk=MOBA_TOPK).reshape(nbd, aw)
    cs_d = state_conv[0]
    lru_d, h_d = _lru_step(xb_d, gy_d, cs_d.transpose(1, 0, 2), state_h[0], cw, cb,
                           wrg_bf, brg, wig_bf, big, lam)
    hd_, hn_d, te_d, gw_d, rk_d, cnt_d = _merge(
        attn_d, lru_d, sga_d, sgl_d, xd2, wao_bf, wlo_bf, wout_bf, g2, wrt, br,
        tm=nbd, n_exp=n_exp, topk=TOP_K)
    y_d = _moe(hd_, hn_d, te_d, gw_d, rk_d, cnt_d, moe_w, rows=2 * SUBLANES, tm_io=nbd, topk=TOP_K)
    conv_d = jnp.concatenate([cs_d, xb_d.reshape(nbd, sd, lw)], axis=1)[:, sd:]

    return (
        y_p.reshape(bp, sp, d),
        y_d.reshape(nbd, sd, d),
        k_p.reshape(1, bp, sp, n_heads, head_dim),
        v_p.reshape(1, bp, sp, n_heads, head_dim),
        hlast_p.reshape(1, bp, lw),
        conv_p.reshape(1, bp, n_tap - 1, lw),
        k_d.reshape(1, nbd, sd, n_heads, head_dim),
        v_d.reshape(1, nbd, sd, n_heads, head_dim),
        h_d.reshape(1, nbd, lw),
        conv_d.reshape(1, nbd, n_tap - 1, lw),
    )
```

```python
import functools

import jax
import jax.numpy as jnp
from jax import lax
from jax.experimental import pallas as pl
from jax.experimental.pallas import tpu as pltpu

F32 = jnp.float32
BF16 = jnp.bfloat16
I32 = jnp.int32

MOBA_BLOCK = 256
MOBA_TOPK = 3
ROPE_THETA = 10000.0
LRU_C = 8.0
TOP_K = 4
SWIGLU_LIMIT = 7.0
SWIGLU_ALPHA = 1.702
NORM_EPS = 1e-6

LANES = 128
SUBLANES = 8
VMEM_LIMIT = 56 * 1024 * 1024

NEG = float(jnp.finfo(jnp.float32).min)
LOG2E = 1.4426950408889634
HIGHEST = lax.Precision.HIGHEST
NT = (((1,), (1,)), ((), ()))


def _cparams(sem):
    return pltpu.CompilerParams(dimension_semantics=sem, vmem_limit_bytes=VMEM_LIMIT)


def _rope_table_kernel(inv_ref, cos_ref, sin_ref, *, pos0, rows, head_dim):
    i = pl.program_id(0)
    pos = lax.broadcasted_iota(I32, (rows, LANES), 0) + (pos0 + i * rows)
    lane = lax.broadcasted_iota(I32, (rows, LANES), 1)
    ang = pos.astype(F32) * inv_ref[...]
    first_half = (lane % head_dim) < (head_dim // 2)
    cos_ref[...] = jnp.cos(ang)
    s = jnp.sin(ang)
    sin_ref[...] = jnp.where(first_half, -s, s)


def _rope_table(n_pos, pos0, head_dim):
    half = head_dim // 2
    inv = ROPE_THETA ** (-jnp.arange(half, dtype=F32) * 2.0 / head_dim)
    inv_l = jnp.tile(inv, LANES // half).reshape(1, LANES)
    n_pad = -(-n_pos // SUBLANES) * SUBLANES
    rows = min(n_pad, 512)
    assert n_pad % rows == 0
    cos, sin = pl.pallas_call(
        functools.partial(_rope_table_kernel, pos0=pos0, rows=rows, head_dim=head_dim),
        grid=(n_pad // rows,),
        in_specs=[pl.BlockSpec((1, LANES), lambda i: (0, 0))],
        out_specs=[pl.BlockSpec((rows, LANES), lambda i: (i, 0))] * 2,
        out_shape=[jax.ShapeDtypeStruct((n_pad, LANES), F32)] * 2,
        name="rope_table",
    )(inv_l)
    return cos[:n_pos], sin[:n_pos]


def _inproj_kernel(x_ref, g1_ref, w_ref, qg_ref, kg_ref, cos_ref, sin_ref, bd_ref,
                   q_ref, k_ref, v_ref, kbf_ref, vt_ref, km_ref, xb_ref, gy_ref, sga_ref, sgl_ref,
                   *, aw, lw, d, head_dim, attn_aux):
    x = x_ref[...]
    xn = x * lax.rsqrt(jnp.mean(x * x, axis=-1, keepdims=True) + NORM_EPS) * g1_ref[...]
    xn = xn.astype(BF16)
    tm = x.shape[0]

    def proj(lo, width):
        return jnp.dot(xn, w_ref[:, lo:lo + width], preferred_element_type=F32)

    reps = aw // LANES
    cos = jnp.concatenate([cos_ref[...]] * reps, axis=1)
    sin = jnp.concatenate([sin_ref[...]] * reps, axis=1)
    lane = lax.broadcasted_iota(I32, (tm, aw), 1)
    first_half = (lane % head_dim) < (head_dim // 2)
    bd = bd_ref[...]

    def head_norm_rope(t, g):
        ss = t * t
        hi = ss.astype(BF16)
        lo = (ss - hi.astype(F32)).astype(BF16)
        ms = (jnp.dot(hi, bd, preferred_element_type=F32)
              + jnp.dot(lo, bd, preferred_element_type=F32))
        tn = t * lax.rsqrt(ms + NORM_EPS) * g
        up = pltpu.roll(tn, aw - head_dim // 2, 1)
        dn = pltpu.roll(tn, head_dim // 2, 1)
        return tn * cos + jnp.where(first_half, up, dn) * sin

    q_ref[...] = head_norm_rope(proj(0, aw), qg_ref[...])
    k = head_norm_rope(proj(aw, aw), kg_ref[...])
    k_ref[...] = k
    v = proj(2 * aw, aw)
    v_ref[...] = v
    if attn_aux:
        kbf_ref[...] = k.astype(BF16)
        km_ref[0] = jnp.mean(k, axis=0, keepdims=True)
        vt_ref[0] = v.T.astype(BF16)
    else:
        kbf_ref[...] = jnp.zeros(kbf_ref.shape, BF16)
        km_ref[...] = jnp.zeros(km_ref.shape, F32)
        vt_ref[...] = jnp.zeros(vt_ref.shape, BF16)
    xb_ref[...] = proj(3 * aw, lw)
    gy_ref[...] = jax.nn.gelu(proj(3 * aw + lw, lw))
    sga_ref[...] = jax.nn.sigmoid(proj(3 * aw + 2 * lw, d))
    sgl_ref[...] = jax.nn.sigmoid(proj(3 * aw + 2 * lw + d, d))


def _inproj(x2d, g1, w_bf, qg, kg, cos, sin, bd, *, tm, n_pos_tiles, aw, lw, head_dim, attn_aux):
    t, d = x2d.shape
    nt = t // tm
    row = lambda i: (i, 0)
    const = lambda i: (0, 0)
    pos = lambda i: (i % n_pos_tiles, 0)
    outs = pl.pallas_call(
        functools.partial(_inproj_kernel, aw=aw, lw=lw, d=d, head_dim=head_dim, attn_aux=attn_aux),
        grid=(nt,),
        in_specs=[
            pl.BlockSpec((tm, d), row),
            pl.BlockSpec((1, d), const),
            pl.BlockSpec(w_bf.shape, const),
            pl.BlockSpec((1, aw), const),
            pl.BlockSpec((1, aw), const),
            pl.BlockSpec((tm, LANES), pos),
            pl.BlockSpec((tm, LANES), pos),
            pl.BlockSpec((aw, aw), const),
        ],
        out_specs=[
            pl.BlockSpec((tm, aw), row),
            pl.BlockSpec((tm, aw), row),
            pl.BlockSpec((tm, aw), row),
            pl.BlockSpec((tm, aw), row),
            pl.BlockSpec((1, aw, tm), lambda i: (i, 0, 0)),
            pl.BlockSpec((1, 1, aw), lambda i: (i, 0, 0)),
            pl.BlockSpec((tm, lw), row),
            pl.BlockSpec((tm, lw), row),
            pl.BlockSpec((tm, d), row),
            pl.BlockSpec((tm, d), row),
        ],
        out_shape=[
            jax.ShapeDtypeStruct((t, aw), F32),
            jax.ShapeDtypeStruct((t, aw), F32),
            jax.ShapeDtypeStruct((t, aw), F32),
            jax.ShapeDtypeStruct((t, aw), BF16),
            jax.ShapeDtypeStruct((nt, aw, tm), BF16),
            jax.ShapeDtypeStruct((nt, 1, aw), F32),
            jax.ShapeDtypeStruct((t, lw), F32),
            jax.ShapeDtypeStruct((t, lw), F32),
            jax.ShapeDtypeStruct((t, d), F32),
            jax.ShapeDtypeStruct((t, d), F32),
        ],
        compiler_params=_cparams(("arbitrary",)),
        name="inproj",
    )(x2d, g1, w_bf, qg, kg, cos, sin, bd)
    return outs


def _top_rows(g, n_rows, topk, valid_floor):
    row = lax.broadcasted_iota(I32, g.shape, 0)
    sel = jnp.zeros(g.shape, dtype=jnp.bool_)
    idxs = []
    vals = []
    for _ in range(topk):
        mx = jnp.max(g, axis=0, keepdims=True)
        idx = jnp.min(jnp.where(g == mx, row, n_rows), axis=0, keepdims=True)
        hit = row == idx
        sel = sel | (hit & (mx > valid_floor))
        g = jnp.where(hit, -jnp.inf, g)
        idxs.append(idx)
        vals.append(mx)
    return idxs, vals, sel


def _attn_kernel(q_ref, k_ref, vt_ref, km_ref, o_ref, sel_ref, *, nb, blk, head_dim, topk):
    i = pl.program_id(2)
    q = q_ref[0]
    n_h = LANES // head_dim
    lane = lax.broadcasted_iota(I32, (blk, LANES), 1)
    brow = lax.broadcasted_iota(I32, (nb, blk), 0)
    qscale = head_dim ** -0.5 * LOG2E
    qs_parts = []
    for hh in range(n_h):
        in_head = (lane >= hh * head_dim) & (lane < (hh + 1) * head_dim)
        qm = jnp.where(in_head, q, 0.0)
        gate = lax.dot_general(km_ref[0].astype(BF16), qm.astype(BF16), NT,
                               preferred_element_type=F32)
        gate = jnp.where(brow < i, gate, NEG)
        _, _, sel = _top_rows(gate, nb, topk, NEG)
        sel_ref[hh] = sel.astype(F32)
        qs_parts.append((qm * qscale).astype(BF16))
    qs = jnp.concatenate(qs_parts, axis=0)

    def scores(j):
        return lax.dot_general(k_ref[0, j], qs, NT, preferred_element_type=F32)

    def update(j, hh, s, m, l, acc):
        m_new = jnp.maximum(m, jnp.max(s, axis=0, keepdims=True))
        alpha = jnp.exp2(m - m_new)
        p = jnp.exp2(s - m_new)
        l = alpha * l + jnp.sum(p, axis=0, keepdims=True)
        pv = jnp.dot(vt_ref[0, j, hh * head_dim:(hh + 1) * head_dim, :], p.astype(BF16),
                     preferred_element_type=F32)
        return m_new, l, alpha * acc + pv

    causal = (lax.broadcasted_iota(I32, (blk, blk), 0) <= lax.broadcasted_iota(I32, (blk, blk), 1))
    s_own = scores(i)
    state = []
    for hh in range(n_h):
        s = jnp.where(causal, s_own[:, hh * blk:(hh + 1) * blk], NEG)
        state.extend(update(i, hh, s, jnp.full((1, blk), NEG, F32), jnp.zeros((1, blk), F32),
                            jnp.zeros((head_dim, blk), F32)))

    def body(j, carry):
        sj = scores(j)
        out = []
        for hh in range(n_h):
            m, l, acc = carry[3 * hh:3 * hh + 3]
            s = jnp.where(sel_ref[hh, pl.ds(j, 1), :] > 0.5, sj[:, hh * blk:(hh + 1) * blk], NEG)
            out.extend(update(j, hh, s, m, l, acc))
        return tuple(out)

    final = lax.fori_loop(0, i, body, tuple(state))
    outs = [final[3 * hh + 2] / final[3 * hh + 1] for hh in range(n_h)]
    o_ref[0] = jnp.concatenate(outs, axis=0).T.astype(BF16)


def _attention(q3, k4, vt4, km3, *, head_dim, topk):
    b, s, aw = q3.shape
    nb, blk = k4.shape[1], k4.shape[2]
    return pl.pallas_call(
        functools.partial(_attn_kernel, nb=nb, blk=blk, head_dim=head_dim, topk=topk),
        grid=(b, aw // LANES, nb),
        in_specs=[
            pl.BlockSpec((1, blk, LANES), lambda bi, hp, i: (bi, i, hp)),
            pl.BlockSpec((1, nb, blk, LANES), lambda bi, hp, i: (bi, 0, 0, hp)),
            pl.BlockSpec((1, nb, LANES, blk), lambda bi, hp, i: (bi, 0, hp, 0)),
            pl.BlockSpec((1, nb, LANES), lambda bi, hp, i: (bi, 0, hp)),
        ],
        out_specs=pl.BlockSpec((1, blk, LANES), lambda bi, hp, i: (bi, i, hp)),
        out_shape=jax.ShapeDtypeStruct((b, s, aw), BF16),
        scratch_shapes=[pltpu.VMEM((LANES // head_dim, nb, blk), F32)],
        compiler_params=_cparams(("arbitrary", "arbitrary", "arbitrary")),
        name="moba_attn",
    )(q3, k4, vt4, km3)


def _col_bcast(row):
    return jnp.broadcast_to(row, (LANES, row.shape[1])).T


def _top_lanes(g, n_cols, topk):
    col = lax.broadcasted_iota(I32, g.shape, 1)
    idxs = []
    for _ in range(topk):
        mx = jnp.max(g, axis=1, keepdims=True)
        idx = jnp.min(jnp.where(g == mx, col, n_cols), axis=1, keepdims=True)
        g = jnp.where(col == idx, -jnp.inf, g)
        idxs.append(idx)
    return idxs


def _dec_gate_kernel(pt_ref, q_ref, *refs, n_pp, ppb, n_groups, blk, topk, head_dim):
    page_refs = refs[:n_pp]
    sel_ref, qcol_sc, g_sc = refs[n_pp:]
    g = pl.program_id(1)
    n_heads = qcol_sc.shape[0]
    gw = g_sc.shape[1]
    per_step = n_pp // ppb

    @pl.when(g == 0)
    def _():
        qc = _col_bcast(q_ref[0])
        for h in range(n_heads):
            qcol_sc[h] = qc[h * head_dim:(h + 1) * head_dim]
        g_sc[...] = jnp.full(g_sc.shape, NEG, F32)

    lane = lax.broadcasted_iota(I32, (n_heads, gw), 1)
    gates = g_sc[...]
    qcol = qcol_sc[:, :, 0:1].astype(BF16).astype(F32)
    for u in range(per_step):
        tot = page_refs[u * ppb][0]
        for w in range(1, ppb):
            tot = tot + page_refs[u * ppb + w][0]
        k_mean = jnp.sum(tot, axis=2, keepdims=True) * (1.0 / blk)
        score = jnp.sum(k_mean.astype(BF16).astype(F32) * qcol, axis=1)
        gates = jnp.where(lane == g * per_step + u, score, gates)
    g_sc[...] = gates

    @pl.when(g == n_groups - 1)
    def _():
        idxs = _top_lanes(g_sc[...], gw, topk)
        olane = lax.broadcasted_iota(I32, (n_heads, LANES), 1)
        out = jnp.zeros((n_heads, LANES), I32)
        for r, idx in enumerate(idxs):
            out = jnp.where(olane == r, idx, out)
        sel_ref[0] = out


def _dec_select(page_table, q3, cache_kt, *, ppb, blk, topk):
    nbd, n_pages = page_table.shape
    _, n_heads, head_dim, page = cache_kt.shape
    aw = n_heads * head_dim
    n_pp = ppb
    while n_pp * 2 <= 16 and n_pages % (n_pp * 2) == 0:
        n_pp *= 2
    n_groups = n_pages // n_pp
    nbp = n_pages // ppb
    gw = -(-nbp // LANES) * LANES

    def page_spec(u):
        return pl.BlockSpec((1, n_heads, head_dim, page),
                            lambda bi, g, pt: (pt[bi, g * n_pp + u], 0, 0, 0))

    grid_spec = pltpu.PrefetchScalarGridSpec(
        num_scalar_prefetch=1,
        grid=(nbd, n_groups),
        in_specs=[pl.BlockSpec((1, 1, aw), lambda bi, g, pt: (bi, 0, 0))]
        + [page_spec(u) for u in range(n_pp)],
        out_specs=pl.BlockSpec((1, n_heads, LANES), lambda bi, g, pt: (bi, 0, 0)),
        scratch_shapes=[pltpu.VMEM((n_heads, head_dim, LANES), F32),
                        pltpu.VMEM((n_heads, gw), F32)],
    )
    return pl.pallas_call(
        functools.partial(_dec_gate_kernel, n_pp=n_pp, ppb=ppb, n_groups=n_groups, blk=blk,
                          topk=topk, head_dim=head_dim),
        grid_spec=grid_spec,
        out_shape=jax.ShapeDtypeStruct((nbd, n_heads, LANES), I32),
        compiler_params=_cparams(("arbitrary", "arbitrary")),
        name="dec_block_select",
    )(page_table, q3, *([cache_kt] * n_pp))


def _dec_attn_kernel(pt_ref, sel_ref, q_ref, kn_ref, vn_ref, *refs, n_heads, head_dim, ppb, topk):
    n_pg = n_heads * ppb
    k_refs = refs[:n_pg]
    v_refs = refs[n_pg:2 * n_pg]
    o_ref, qcol_sc, m_sc, l_sc, acc_sc = refs[2 * n_pg:]
    r = pl.program_id(1)
    scale = head_dim ** -0.5
    lane = lax.broadcasted_iota(I32, (head_dim, LANES), 1)

    @pl.when(r == 0)
    def _():
        qc = _col_bcast(q_ref[0])
        kc = _col_bcast(kn_ref[0])
        vc = _col_bcast(vn_ref[0])
        for h in range(n_heads):
            rows = slice(h * head_dim, (h + 1) * head_dim)
            qcol_sc[h] = qc[rows]
            m_sc[h:h + 1, :] = jnp.sum(qc[rows] * kc[rows], axis=0, keepdims=True) * scale
            l_sc[h:h + 1, :] = jnp.ones((1, LANES), F32)
            acc_sc[h] = jnp.where(lane == 0, vc[rows], 0.0)

    for h in range(n_heads):
        qc = qcol_sc[h]
        m = m_sc[h:h + 1, :]
        ss = [jnp.sum(k_refs[h * ppb + w][0, 0] * qc, axis=0, keepdims=True) * scale
              for w in range(ppb)]
        blk_max = ss[0]
        for s in ss[1:]:
            blk_max = jnp.maximum(blk_max, s)
        m_new = jnp.maximum(m, jnp.max(blk_max, axis=1, keepdims=True))
        alpha = jnp.exp(m - m_new)
        l = alpha * l_sc[h:h + 1, :]
        acc = alpha * acc_sc[h]
        for w in range(ppb):
            p = jnp.exp(ss[w] - m_new)
            l = l + jnp.sum(p, axis=1, keepdims=True)
            acc = acc + v_refs[h * ppb + w][0, 0] * p
        m_sc[h:h + 1, :] = m_new
        l_sc[h:h + 1, :] = l
        acc_sc[h] = acc

    @pl.when(r == topk - 1)
    def _():
        cols = [jnp.broadcast_to(
            jnp.sum(acc_sc[h], axis=1, keepdims=True) / l_sc[h:h + 1, 0:1], (head_dim, LANES))
            for h in range(n_heads)]
        o_ref[0] = jnp.concatenate(cols, axis=0).T[0:1]


def _dec_attention(page_table, sel, q3, kn3, vn3, cache_kt, cache_vt, *, ppb, topk):
    nbd = page_table.shape[0]
    _, n_heads, head_dim, page = cache_kt.shape
    aw = n_heads * head_dim

    def page_spec(h, w):
        return pl.BlockSpec(
            (1, 1, head_dim, page),
            lambda bi, r, pt, sl: (pt[bi, ppb * sl[bi, h, r] + w], h, 0, 0))

    vec = pl.BlockSpec((1, 1, aw), lambda bi, r, pt, sl: (bi, 0, 0))
    pages = [page_spec(h, w) for h in range(n_heads) for w in range(ppb)]
    grid_spec = pltpu.PrefetchScalarGridSpec(
        num_scalar_prefetch=2,
        grid=(nbd, topk),
        in_specs=[vec, vec, vec] + pages + pages,
        out_specs=vec,
        scratch_shapes=[pltpu.VMEM((n_heads, head_dim, LANES), F32),
                        pltpu.VMEM((n_heads, LANES), F32),
                        pltpu.VMEM((n_heads, LANES), F32),
                        pltpu.VMEM((n_heads, head_dim, LANES), F32)],
    )
    n_pg = n_heads * ppb
    return pl.pallas_call(
        functools.partial(_dec_attn_kernel, n_heads=n_heads, head_dim=head_dim, ppb=ppb, topk=topk),
        grid_spec=grid_spec,
        out_shape=jax.ShapeDtypeStruct((nbd, 1, aw), F32),
        compiler_params=_cparams(("arbitrary", "arbitrary")),
        name="dec_attn",
    )(page_table, sel, q3, kn3, vn3, *([cache_kt] * n_pg), *([cache_vt] * n_pg))


def _lru_gates(xc, wrg_ref, brg, wig_ref, big, lam, first_pos_mask):
    nblk, bw = wrg_ref.shape[0], wrg_ref.shape[1]
    xcb = xc.astype(BF16)
    rz = jnp.concatenate(
        [jnp.dot(xcb[:, n * bw:(n + 1) * bw], wrg_ref[n], preferred_element_type=F32)
         for n in range(nblk)], axis=1)
    iz = jnp.concatenate(
        [jnp.dot(xcb[:, n * bw:(n + 1) * bw], wig_ref[n], preferred_element_type=F32)
         for n in range(nblk)], axis=1)
    r = jax.nn.sigmoid(rz + brg)
    ig = jax.nn.sigmoid(iz + big)
    neg_lam = -lam
    softplus = jnp.maximum(neg_lam, 0.0) + jnp.log1p(jnp.exp(-jnp.abs(neg_lam)))
    log_a = -LRU_C * r * softplus
    a = jnp.exp(log_a)
    mult = jnp.sqrt(jnp.tanh(-log_a) * (a * a + 1.0))
    if first_pos_mask is not None:
        mult = jnp.where(first_pos_mask, 1.0, mult)
    return a, xc * ig * mult


def _lru_kernel(xb_ref, gy_ref, cs_ref, h0_ref, cw_ref, cb_ref, wrg_ref, brg_ref, wig_ref, big_ref,
                lam_ref, out_ref, hlast_ref, xext, a_sc, u_sc, hcar, *, ts, chunk, pos0):
    t = pl.program_id(1)
    w = xb_ref.shape[1]
    halo = SUBLANES
    n_tap = cw_ref.shape[0]

    @pl.when(t == 0)
    def _():
        xext[0:halo, :] = jnp.zeros((halo, w), F32)
        xext[halo - (n_tap - 1):halo, :] = cs_ref[0]
        hcar[...] = h0_ref[0]

    xext[halo:halo + ts, :] = xb_ref[...]
    xc = cb_ref[...]
    for j in range(n_tap):
        lo = halo - (n_tap - 1) + j
        xc = xc + xext[lo:lo + ts, :] * cw_ref[j:j + 1, :]
    xext[0:halo, :] = xext[ts:ts + halo, :]

    row = lax.broadcasted_iota(I32, (ts, w), 0)
    first = (row + t * ts + pos0) == 0
    a, u = _lru_gates(xc, wrg_ref, brg_ref[...], wig_ref, big_ref[...], lam_ref[...], first)
    n_lc = w // LANES
    for lc in range(n_lc):
        a_sc[lc] = a[:, lc * LANES:(lc + 1) * LANES]
        u_sc[lc] = u[:, lc * LANES:(lc + 1) * LANES]

    hs = [jnp.zeros((SUBLANES, LANES), F32)] * n_lc
    prods = [jnp.ones((SUBLANES, LANES), F32)] * n_lc
    for l in range(chunk):
        idx = pl.ds(l, SUBLANES, stride=chunk)
        for lc in range(n_lc):
            a8 = a_sc[lc, idx, :]
            hs[lc] = a8 * hs[lc] + u_sc[lc, idx, :]
            prods[lc] = a8 * prods[lc]
            u_sc[lc, idx, :] = hs[lc]
            a_sc[lc, idx, :] = prods[lc]

    carry = hcar[...]
    for c in range(SUBLANES):
        rows = slice(c * chunk, (c + 1) * chunk)
        ends = []
        for lc in range(n_lc):
            cols = slice(lc * LANES, (lc + 1) * LANES)
            h_true = u_sc[lc, rows, :] + a_sc[lc, rows, :] * carry[:, cols]
            out_ref[rows, cols] = (h_true * gy_ref[rows, cols]).astype(BF16)
            ends.append(h_true[chunk - 1:chunk, :])
        carry = jnp.concatenate(ends, axis=1)
    hcar[...] = carry
    hlast_ref[0] = carry


def _lru_prompt(xb, gy, cs, h0, cw, cb, wrg, brg, wig, big, lam, *, b, s, ts, pos0):
    w = xb.shape[1]
    nt = s // ts
    chunk = ts // SUBLANES
    n_tap = cw.shape[0]
    row = lambda bi, t: (bi * nt + t, 0)
    const2 = lambda bi, t: (0, 0)
    const3 = lambda bi, t: (0, 0, 0)
    per_b = lambda bi, t: (bi, 0, 0)
    return pl.pallas_call(
        functools.partial(_lru_kernel, ts=ts, chunk=chunk, pos0=pos0),
        grid=(b, nt),
        in_specs=[
            pl.BlockSpec((ts, w), row),
            pl.BlockSpec((ts, w), row),
            pl.BlockSpec((1, n_tap - 1, w), per_b),
            pl.BlockSpec((1, 1, w), per_b),
            pl.BlockSpec(cw.shape, const2),
            pl.BlockSpec((1, w), const2),
            pl.BlockSpec(wrg.shape, const3),
            pl.BlockSpec((1, w), const2),
            pl.BlockSpec(wig.shape, const3),
            pl.BlockSpec((1, w), const2),
            pl.BlockSpec((1, w), const2),
        ],
        out_specs=[pl.BlockSpec((ts, w), row), pl.BlockSpec((1, 1, w), per_b)],
        out_shape=[jax.ShapeDtypeStruct((b * s, w), BF16), jax.ShapeDtypeStruct((b, 1, w), F32)],
        scratch_shapes=[
            pltpu.VMEM((ts + SUBLANES, w), F32),
            pltpu.VMEM((w // LANES, ts, LANES), F32),
            pltpu.VMEM((w // LANES, ts, LANES), F32),
            pltpu.VMEM((1, w), F32),
        ],
        compiler_params=_cparams(("arbitrary", "arbitrary")),
        name="rglru_scan",
    )(xb, gy, cs, h0, cw, cb, wrg, brg, wig, big, lam)


def _lru_step_kernel(xb_ref, gy_ref, cs_ref, h0_ref, cw_ref, cb_ref, wrg_ref, brg_ref, wig_ref,
                     big_ref, lam_ref, out_ref, h_ref):
    n_tap = cw_ref.shape[0]
    xc = cb_ref[...]
    for j in range(n_tap - 1):
        xc = xc + cs_ref[j] * cw_ref[j:j + 1, :]
    xc = xc + xb_ref[...] * cw_ref[n_tap - 1:n_tap, :]
    a, u = _lru_gates(xc, wrg_ref, brg_ref[...], wig_ref, big_ref[...], lam_ref[...], None)
    h = a * h0_ref[...] + u
    h_ref[...] = h
    out_ref[...] = h * gy_ref[...]


def _lru_step(xb, gy, cs_t, h0, cw, cb, wrg, brg, wig, big, lam):
    n, w = xb.shape
    return pl.pallas_call(
        _lru_step_kernel,
        out_shape=[jax.ShapeDtypeStruct((n, w), F32), jax.ShapeDtypeStruct((n, w), F32)],
        name="rglru_step",
    )(xb, gy, cs_t, h0, cw, cb, wrg, brg, wig, big, lam)


def _merge_kernel(attn_ref, lru_ref, sga_ref, sgl_ref, x_ref, wao_ref, wlo_ref, wout_ref, g2_ref,
                  wrt_ref, br_ref, h_ref, hn_ref, te_ref, gw_ref, rk_ref, cnt_ref, carry_sc,
                  *, n_exp, topk):
    i = pl.program_id(0)
    tm = x_ref.shape[0]

    @pl.when(i == 0)
    def _():
        carry_sc[...] = jnp.zeros(carry_sc.shape, F32)

    att = jnp.dot(attn_ref[...].astype(BF16), wao_ref[...], preferred_element_type=F32)
    lru = jnp.dot(lru_ref[...].astype(BF16), wlo_ref[...], preferred_element_type=F32)
    mixed = sga_ref[...] * att + sgl_ref[...] * lru
    h = x_ref[...] + jnp.dot(mixed.astype(BF16), wout_ref[...], preferred_element_type=F32)
    h_ref[...] = h
    hn = h * lax.rsqrt(jnp.mean(h * h, axis=-1, keepdims=True) + NORM_EPS) * g2_ref[...]
    hn_ref[...] = hn

    logits = lax.dot_general(wrt_ref[...], hn.astype(BF16), NT,
                             preferred_element_type=F32) + br_ref[...]
    idxs, vals, sel = _top_rows(logits, n_exp, topk, -jnp.inf)
    top_v = jnp.concatenate(vals, axis=0)
    ex = jnp.exp(top_v - top_v[0:1])
    gw_ref[...] = ex / jnp.sum(ex, axis=0, keepdims=True)
    te_ref[...] = jnp.concatenate(idxs, axis=0)

    before = lax.broadcasted_iota(I32, (tm, tm), 0) < lax.broadcasted_iota(I32, (tm, tm), 1)
    sel_bf = sel.astype(BF16)
    prior = jnp.dot(sel_bf, before.astype(BF16), preferred_element_type=F32)
    base = prior + carry_sc[:, 0:1]
    erow = lax.broadcasted_iota(I32, (n_exp, tm), 0)
    ranks = [jnp.sum(jnp.where(erow == idx, base, 0.0), axis=0, keepdims=True) for idx in idxs]
    rk_ref[...] = jnp.concatenate(ranks, axis=0).astype(I32)
    carry_sc[...] = carry_sc[...] + jnp.sum(sel.astype(F32), axis=1, keepdims=True)
    cnt_ref[...] = carry_sc[...]


def _merge(attn, lru, sga, sgl, x2d, wao, wlo, wout, g2, wrt, br, *, tm, n_exp, topk):
    t, d = x2d.shape
    aw, lw = attn.shape[1], lru.shape[1]
    row = lambda i: (i, 0)
    col = lambda i: (0, i)
    const = lambda i: (0, 0)
    return pl.pallas_call(
        functools.partial(_merge_kernel, n_exp=n_exp, topk=topk),
        grid=(t // tm,),
        in_specs=[
            pl.BlockSpec((tm, aw), row),
            pl.BlockSpec((tm, lw), row),
            pl.BlockSpec((tm, d), row),
            pl.BlockSpec((tm, d), row),
            pl.BlockSpec((tm, d), row),
            pl.BlockSpec((aw, d), const),
            pl.BlockSpec((lw, d), const),
            pl.BlockSpec((d, d), const),
            pl.BlockSpec((1, d), const),
            pl.BlockSpec((n_exp, d), const),
            pl.BlockSpec((n_exp, 1), const),
        ],
        out_specs=[
            pl.BlockSpec((tm, d), row),
            pl.BlockSpec((tm, d), row),
            pl.BlockSpec((topk, tm), col),
            pl.BlockSpec((topk, tm), col),
            pl.BlockSpec((topk, tm), col),
            pl.BlockSpec((n_exp, LANES), const),
        ],
        out_shape=[
            jax.ShapeDtypeStruct((t, d), F32),
            jax.ShapeDtypeStruct((t, d), F32),
            jax.ShapeDtypeStruct((topk, t), I32),
            jax.ShapeDtypeStruct((topk, t), F32),
            jax.ShapeDtypeStruct((topk, t), I32),
            jax.ShapeDtypeStruct((n_exp, LANES), F32),
        ],
        scratch_shapes=[pltpu.VMEM((n_exp, LANES), F32)],
        compiler_params=_cparams(("arbitrary",)),
        name="merge_router",
    )(attn, lru, sga, sgl, x2d, wao, wlo, wout, g2, wrt, br)


def _row_copy(src, src_row, dst, dst_row, sem):
    return pltpu.make_async_copy(src.at[pl.ds(src_row, 1)], dst.at[pl.ds(dst_row, 1)], sem)


def _dispatch_kernel(dest_ref, hn_ref, buf_in, buf_out, sem, *, topk):
    del buf_in
    tm = hn_ref.shape[0]

    def issue(t, c):
        for k in range(topk):
            _row_copy(hn_ref, t, buf_out, dest_ref[k, t], sem).start()
        return c

    lax.fori_loop(0, tm, issue, 0)

    def drain(t, c):
        for k in range(topk):
            _row_copy(hn_ref, 0, buf_out, 0, sem).wait()
        return c

    lax.fori_loop(0, tm, drain, 0)


def _dispatch(dest, hn, n_rows, *, tm, topk):
    t, d = hn.shape
    buf0 = jnp.zeros((n_rows, d), hn.dtype)
    return pl.pallas_call(
        functools.partial(_dispatch_kernel, topk=topk),
        grid=(t // tm,),
        in_specs=[
            pl.BlockSpec((topk, tm), lambda i: (0, i), memory_space=pltpu.SMEM),
            pl.BlockSpec((tm, d), lambda i: (i, 0)),
            pl.BlockSpec(memory_space=pl.ANY),
        ],
        out_specs=pl.BlockSpec(memory_space=pl.ANY),
        out_shape=jax.ShapeDtypeStruct((n_rows, d), hn.dtype),
        scratch_shapes=[pltpu.SemaphoreType.DMA(())],
        input_output_aliases={2: 0},
        compiler_params=_cparams(("arbitrary",)),
        name="moe_dispatch",
    )(dest, hn, buf0)


def _expert_kernel(be_ref, nu_ref, x_ref, wg_ref, wl_ref, wd_ref, bg_ref, bl_ref, bd_ref, o_ref):
    i = pl.program_id(0)

    @pl.when(i < nu_ref[0])
    def _():
        x = x_ref[...].astype(BF16)
        g = jnp.dot(x, wg_ref[0], preferred_element_type=F32) + bg_ref[0]
        lin = jnp.dot(x, wl_ref[0], preferred_element_type=F32) + bl_ref[0]
        glu = jnp.minimum(g, SWIGLU_LIMIT)
        lin = jnp.clip(lin, -SWIGLU_LIMIT, SWIGLU_LIMIT)
        act = glu * jax.nn.sigmoid(SWIGLU_ALPHA * glu) * (lin + 1.0)
        o_ref[...] = jnp.dot(act.astype(BF16), wd_ref[0], preferred_element_type=F32) + bd_ref[0]

    @pl.when(i >= nu_ref[0])
    def _():
        o_ref[...] = jnp.zeros(o_ref.shape, F32)


def _experts(blk_e, n_used, buf, wg, wl, wd, bg, bl, bd, *, rows):
    n_rows, d = buf.shape
    de = wg.shape[2]
    xrow = lambda i, be, nu: (i, 0)
    wsel = lambda i, be, nu: (be[i], 0, 0)
    grid_spec = pltpu.PrefetchScalarGridSpec(
        num_scalar_prefetch=2,
        grid=(n_rows // rows,),
        in_specs=[
            pl.BlockSpec((rows, d), xrow),
            pl.BlockSpec((1, d, de), wsel),
            pl.BlockSpec((1, d, de), wsel),
            pl.BlockSpec((1, de, d), wsel),
            pl.BlockSpec((1, 1, de), wsel),
            pl.BlockSpec((1, 1, de), wsel),
            pl.BlockSpec((1, 1, d), wsel),
        ],
        out_specs=pl.BlockSpec((rows, d), xrow),
    )
    return pl.pallas_call(
        _expert_kernel,
        grid_spec=grid_spec,
        out_shape=jax.ShapeDtypeStruct((n_rows, d), F32),
        compiler_params=_cparams(("arbitrary",)),
        name="moe_experts",
    )(blk_e, n_used, buf, wg, wl, wd, bg, bl, bd)


def _combine_kernel(dest_ref, gw_ref, h_ref, obuf, y_ref, g_sc, sem, *, topk):
    tm = h_ref.shape[0]

    def issue(t, c):
        for k in range(topk):
            _row_copy(obuf, dest_ref[k, t], g_sc.at[k], t, sem).start()
        return c

    lax.fori_loop(0, tm, issue, 0)

    def drain(t, c):
        for k in range(topk):
            _row_copy(obuf, 0, g_sc.at[k], 0, sem).wait()
        return c

    lax.fori_loop(0, tm, drain, 0)
    gw = gw_ref[...]
    ff = g_sc[0] * gw[:, 0:1]
    for k in range(1, topk):
        ff = ff + g_sc[k] * gw[:, k:k + 1]
    y_ref[...] = h_ref[...] + ff


def _combine(dest, gw_t, h, obuf, *, tm, topk):
    t, d = h.shape
    return pl.pallas_call(
        functools.partial(_combine_kernel, topk=topk),
        grid=(t // tm,),
        in_specs=[
            pl.BlockSpec((topk, tm), lambda i: (0, i), memory_space=pltpu.SMEM),
            pl.BlockSpec((tm, topk), lambda i: (i, 0)),
            pl.BlockSpec((tm, d), lambda i: (i, 0)),
            pl.BlockSpec(memory_space=pl.ANY),
        ],
        out_specs=pl.BlockSpec((tm, d), lambda i: (i, 0)),
        out_shape=jax.ShapeDtypeStruct((t, d), F32),
        scratch_shapes=[pltpu.VMEM((topk, tm, d), F32), pltpu.SemaphoreType.DMA(())],
        compiler_params=_cparams(("arbitrary",)),
        name="moe_combine",
    )(dest, gw_t, h, obuf)


def _split_gu_kernel(w_ref, wg_ref, wl_ref, *, seg):
    x = w_ref[0].astype(BF16)
    src = lax.broadcasted_iota(I32, (2 * seg, 2 * seg), 0)
    dst = lax.broadcasted_iota(I32, (2 * seg, 2 * seg), 1)
    wanted = jnp.where(dst < seg, 2 * dst, 2 * (dst - seg) + 1)
    pick = (src == wanted).astype(BF16)
    for n in range(x.shape[1] // (2 * seg)):
        both = jnp.dot(x[:, n * 2 * seg:(n + 1) * 2 * seg], pick, preferred_element_type=F32)
        wg_ref[0, :, n * seg:(n + 1) * seg] = both[:, :seg].astype(BF16)
        wl_ref[0, :, n * seg:(n + 1) * seg] = both[:, seg:].astype(BF16)


def _split_gu(w_gu):
    n_exp, d, de2 = w_gu.shape
    rt = min(d, 512)
    seg = 256
    assert d % rt == 0 and de2 % (2 * seg) == 0
    return pl.pallas_call(
        functools.partial(_split_gu_kernel, seg=seg),
        grid=(n_exp, d // rt),
        in_specs=[pl.BlockSpec((1, rt, de2), lambda e, r: (e, r, 0))],
        out_specs=[pl.BlockSpec((1, rt, de2 // 2), lambda e, r: (e, r, 0))] * 2,
        out_shape=[jax.ShapeDtypeStruct((n_exp, d, de2 // 2), BF16)] * 2,
        compiler_params=_cparams(("arbitrary", "arbitrary")),
        name="split_gu",
    )(w_gu)


def _dest_kernel(ps_ref, te_ref, rk_ref, dest_ref, *, n_exp):
    te = te_ref[...]
    dest = rk_ref[...]
    for e in range(n_exp):
        dest = dest + jnp.where(te == e, ps_ref[e], 0)
    dest_ref[...] = dest


def _dest_rows(pad_start, te, rk):
    return pl.pallas_call(
        functools.partial(_dest_kernel, n_exp=pad_start.shape[0]),
        in_specs=[pl.BlockSpec(memory_space=pltpu.SMEM),
                  pl.BlockSpec(memory_space=pltpu.VMEM),
                  pl.BlockSpec(memory_space=pltpu.VMEM)],
        out_specs=pl.BlockSpec(memory_space=pltpu.VMEM),
        out_shape=jax.ShapeDtypeStruct(te.shape, I32),
        name="moe_dest",
    )(pad_start, te, rk)


def _moe(h, hn, te, gw, rk, cnt, wts, *, rows, tm_io, topk):
    t, d = h.shape
    wg, wl, wd, bg, bl, bd = wts
    n_exp = wg.shape[0]
    counts = cnt[:, 0].astype(I32)
    padded = (counts + rows - 1) // rows * rows
    pad_end = jnp.cumsum(padded)
    pad_start = pad_end - padded
    dest = _dest_rows(pad_start, te, rk)
    n_blk = -(-(t * topk) // rows) + n_exp
    blk_lo = jnp.arange(n_blk, dtype=I32) * rows
    blk_e = jnp.minimum(
        jnp.sum((pad_end[None, :] <= blk_lo[:, None]).astype(I32), axis=1), n_exp - 1)
    n_used = (pad_end[-1:] // rows).astype(I32)
    buf = _dispatch(dest, hn, n_blk * rows, tm=tm_io, topk=topk)
    obuf = _experts(blk_e, n_used, buf, wg, wl, wd, bg, bl, bd, rows=rows)
    return _combine(dest, gw.T, h, obuf, tm=tm_io, topk=topk)


def kernel(x_prompt, x_sample, cache_k, cache_v, state_h, state_conv, page_table, norm1_g, w_in,
           q_norm_g, k_norm_g, conv_w, conv_b, w_rg, b_rg, w_ig, b_ig, lru_lambda, w_attn_o,
           w_lru_o, w_out, norm2_g, w_router, b_router, w_gu, b_gu, w_dn, b_dn):
    depth = w_in.shape[0]
    assert depth == 1, "one layer per step is supported"
    bp, sp, d = x_prompt.shape
    nbd, sd, _ = x_sample.shape
    assert sd == 1, "decode handles one new token per sequence"
    page, n_heads, head_dim = cache_k.shape[2], cache_k.shape[3], cache_k.shape[4]
    aw = n_heads * head_dim
    lw = w_lru_o.shape[1]
    n_exp = w_router.shape[2]
    n_pages = page_table.shape[1]
    past_len = n_pages * page
    blk = MOBA_BLOCK
    assert sp % blk == 0 and past_len % blk == 0 and blk % page == 0
    assert LANES % head_dim == 0 and aw % LANES == 0 and page == LANES
    ppb = blk // page
    assert past_len // blk >= MOBA_TOPK

    w_in_bf = w_in[0].astype(BF16)
    wrg_bf, wig_bf = w_rg[0].astype(BF16), w_ig[0].astype(BF16)
    wao_bf, wlo_bf, wout_bf = w_attn_o[0].astype(BF16), w_lru_o[0].astype(BF16), w_out[0].astype(BF16)
    wg_bf, wl_bf = _split_gu(w_gu[0])
    wd_bf = w_dn[0].astype(BF16)
    bg = b_gu[0][:, None, 0::2]
    bl = b_gu[0][:, None, 1::2]
    bdn = b_dn[0][:, None, :]
    moe_w = (wg_bf, wl_bf, wd_bf, bg, bl, bdn)
    qg = jnp.tile(q_norm_g[0], n_heads).reshape(1, aw)
    kg = jnp.tile(k_norm_g[0], n_heads).reshape(1, aw)
    head_of = jnp.arange(aw, dtype=I32) // head_dim
    same_head = head_of[:, None] == head_of[None, :]
    bd_mean = (same_head.astype(F32) / head_dim).astype(BF16)
    g1 = norm1_g[0].reshape(1, d)
    g2 = norm2_g[0].reshape(1, d)
    wrt = w_router[0].T.astype(BF16)
    br = b_router[0].reshape(n_exp, 1)
    cw, cb = conv_w[0], conv_b[0].reshape(1, lw)
    brg, big, lam = b_rg[0].reshape(1, lw), b_ig[0].reshape(1, lw), lru_lambda[0].reshape(1, lw)
    n_tap = cw.shape[0]

    tp = bp * sp
    cos_p, sin_p = _rope_table(sp, 0, head_dim)
    xp2 = x_prompt.reshape(tp, d)
    (q_p, k_p, v_p, kbf_p, vt_p, km_p, xb_p, gy_p, sga_p, sgl_p) = _inproj(
        xp2, g1, w_in_bf, qg, kg, cos_p, sin_p, bd_mean,
        tm=blk, n_pos_tiles=sp // blk, aw=aw, lw=lw, head_dim=head_dim, attn_aux=True)
    nb = sp // blk
    attn_p = _attention(
        q_p.reshape(bp, sp, aw), kbf_p.reshape(bp, nb, blk, aw), vt_p.reshape(bp, nb, aw, blk),
        km_p.reshape(bp, nb, aw), head_dim=head_dim, topk=MOBA_TOPK).reshape(tp, aw)
    lru_p, hlast_p = _lru_prompt(
        xb_p, gy_p, jnp.zeros((bp, n_tap - 1, lw), F32), jnp.zeros((bp, 1, lw), F32),
        cw, cb, wrg_bf, brg, wig_bf, big, lam, b=bp, s=sp, ts=blk, pos0=0)
    h_p, hn_p, te_p, gw_p, rk_p, cnt_p = _merge(
        attn_p, lru_p, sga_p, sgl_p, xp2, wao_bf, wlo_bf, wout_bf, g2, wrt, br,
        tm=blk, n_exp=n_exp, topk=TOP_K)
    y_p = _moe(h_p, hn_p, te_p, gw_p, rk_p, cnt_p, moe_w, rows=256, tm_io=blk, topk=TOP_K)
    assert sp >= n_tap - 1
    conv_p = xb_p.reshape(bp, sp, lw)[:, sp - (n_tap - 1):]

    cos_d, sin_d = _rope_table(sd, past_len, head_dim)
    cos_d = jnp.tile(cos_d, (nbd, 1))
    sin_d = jnp.tile(sin_d, (nbd, 1))
    xd2 = x_sample.reshape(nbd, d)
    (q_d, k_d, v_d, _, _, _, xb_d, gy_d, sga_d, sgl_d) = _inproj(
        xd2, g1, w_in_bf, qg, kg, cos_d, sin_d, bd_mean,
        tm=nbd, n_pos_tiles=1, aw=aw, lw=lw, head_dim=head_dim, attn_aux=False)
    ckt = cache_k[0].transpose(0, 2, 3, 1)
    cvt = cache_v[0].transpose(0, 2, 3, 1)
    q_d3 = q_d.reshape(nbd, 1, aw)
    sel = _dec_select(page_table, q_d3, ckt, ppb=ppb, blk=blk, topk=MOBA_TOPK)[:, :, :MOBA_TOPK]
    attn_d = _dec_attention(
        page_table, sel, q_d3, k_d.reshape(nbd, 1, aw), v_d.reshape(nbd, 1, aw), ckt, cvt,
        ppb=ppb, topk=MOBA_TOPK).reshape(nbd, aw)
    cs_d = state_conv[0]
    lru_d, h_d = _lru_step(xb_d, gy_d, cs_d.transpose(1, 0, 2), state_h[0], cw, cb,
                           wrg_bf, brg, wig_bf, big, lam)
    hd_, hn_d, te_d, gw_d, rk_d, cnt_d = _merge(
        attn_d, lru_d, sga_d, sgl_d, xd2, wao_bf, wlo_bf, wout_bf, g2, wrt, br,
        tm=nbd, n_exp=n_exp, topk=TOP_K)
    y_d = _moe(hd_, hn_d, te_d, gw_d, rk_d, cnt_d, moe_w, rows=2 * SUBLANES, tm_io=nbd, topk=TOP_K)
    conv_d = jnp.concatenate([cs_d, xb_d.reshape(nbd, sd, lw)], axis=1)[:, sd:]

    return (
        y_p.reshape(bp, sp, d),
        y_d.reshape(nbd, sd, d),
        k_p.reshape(1, bp, sp, n_heads, head_dim),
        v_p.reshape(1, bp, sp, n_heads, head_dim),
        hlast_p.reshape(1, bp, lw),
        conv_p.reshape(1, bp, n_tap - 1, lw),
        k_d.reshape(1, nbd, sd, n_heads, head_dim),
        v_d.reshape(1, nbd, sd, n_heads, head_dim),
        h_d.reshape(1, nbd, lw),
        conv_d.reshape(1, nbd, n_tap - 1, lw),
    )
```

```python
import functools

import jax
import jax.numpy as jnp
from jax import lax
from jax.experimental import pallas as pl
from jax.experimental.pallas import tpu as pltpu

F32 = jnp.float32
BF16 = jnp.bfloat16
I32 = jnp.int32

MOBA_BLOCK = 256
MOBA_TOPK = 3
ROPE_THETA = 10000.0
LRU_C = 8.0
TOP_K = 4
SWIGLU_LIMIT = 7.0
SWIGLU_ALPHA = 1.702
NORM_EPS = 1e-6

LANES = 128
SUBLANES = 8
VMEM_LIMIT = 56 * 1024 * 1024
DMA_UNROLL = 8
NEG = float(jnp.finfo(jnp.float32).min)
LOG2E = 1.4426950408889634
HIGHEST = lax.Precision.HIGHEST
NT = (((1,), (1,)), ((), ()))


def _cparams(sem):
    return pltpu.CompilerParams(dimension_semantics=sem, vmem_limit_bytes=VMEM_LIMIT)


def _rope_table_kernel(inv_ref, cos_ref, sin_ref, *, pos0, rows, head_dim):
    i = pl.program_id(0)
    pos = lax.broadcasted_iota(I32, (rows, LANES), 0) + (pos0 + i * rows)
    lane = lax.broadcasted_iota(I32, (rows, LANES), 1)
    ang = pos.astype(F32) * inv_ref[...]
    first_half = (lane % head_dim) < (head_dim // 2)
    cos_ref[...] = jnp.cos(ang)
    s = jnp.sin(ang)
    sin_ref[...] = jnp.where(first_half, -s, s)


def _rope_table(n_pos, pos0, head_dim):
    half = head_dim // 2
    inv = ROPE_THETA ** (-jnp.arange(half, dtype=F32) * 2.0 / head_dim)
    inv_l = jnp.tile(inv, LANES // half).reshape(1, LANES)
    n_pad = -(-n_pos // SUBLANES) * SUBLANES
    rows = min(n_pad, 512)
    assert n_pad % rows == 0
    cos, sin = pl.pallas_call(
        functools.partial(_rope_table_kernel, pos0=pos0, rows=rows, head_dim=head_dim),
        grid=(n_pad // rows,),
        in_specs=[pl.BlockSpec((1, LANES), lambda i: (0, 0))],
        out_specs=[pl.BlockSpec((rows, LANES), lambda i: (i, 0))] * 2,
        out_shape=[jax.ShapeDtypeStruct((n_pad, LANES), F32)] * 2,
        name="rope_table",
    )(inv_l)
    return cos[:n_pos], sin[:n_pos]


def _inproj_kernel(x_ref, g1_ref, w_ref, qg_ref, kg_ref, cos_ref, sin_ref, bd_ref,
                   q_ref, k_ref, v_ref, kbf_ref, vt_ref, km_ref, xb_ref, gy_ref, sga_ref, sgl_ref,
                   *, aw, lw, d, head_dim, attn_aux):
    x = x_ref[...]
    xn = x * lax.rsqrt(jnp.mean(x * x, axis=-1, keepdims=True) + NORM_EPS) * g1_ref[...]
    xn = xn.astype(BF16)
    tm = x.shape[0]

    def proj(lo, width):
        return jnp.dot(xn, w_ref[:, lo:lo + width], preferred_element_type=F32)

    reps = aw // LANES
    cos = jnp.concatenate([cos_ref[...]] * reps, axis=1)
    sin = jnp.concatenate([sin_ref[...]] * reps, axis=1)
    lane = lax.broadcasted_iota(I32, (tm, aw), 1)
    first_half = (lane % head_dim) < (head_dim // 2)
    bd = bd_ref[...]

    def head_norm_rope(t, g):
        ss = t * t
        hi = ss.astype(BF16)
        lo = (ss - hi.astype(F32)).astype(BF16)
        ms = (jnp.dot(hi, bd, preferred_element_type=F32)
              + jnp.dot(lo, bd, preferred_element_type=F32))
        tn = t * lax.rsqrt(ms + NORM_EPS) * g
        up = pltpu.roll(tn, aw - head_dim // 2, 1)
        dn = pltpu.roll(tn, head_dim // 2, 1)
        return tn * cos + jnp.where(first_half, up, dn) * sin

    q_ref[...] = head_norm_rope(proj(0, aw), qg_ref[...])
    k = head_norm_rope(proj(aw, aw), kg_ref[...])
    k_ref[...] = k
    v = proj(2 * aw, aw)
    v_ref[...] = v
    if attn_aux:
        kbf_ref[...] = k.astype(BF16)
        km_ref[0] = jnp.mean(k, axis=0, keepdims=True)
        vt_ref[0] = v.T.astype(BF16)
    else:
        kbf_ref[...] = jnp.zeros(kbf_ref.shape, BF16)
        km_ref[...] = jnp.zeros(km_ref.shape, F32)
        vt_ref[...] = jnp.zeros(vt_ref.shape, BF16)
    xb_ref[...] = proj(3 * aw, lw)
    gy_ref[...] = jax.nn.gelu(proj(3 * aw + lw, lw))
    sga_ref[...] = jax.nn.sigmoid(proj(3 * aw + 2 * lw, d))
    sgl_ref[...] = jax.nn.sigmoid(proj(3 * aw + 2 * lw + d, d))


def _inproj(x2d, g1, w_bf, qg, kg, cos, sin, bd, *, tm, n_pos_tiles, aw, lw, head_dim, attn_aux):
    t, d = x2d.shape
    nt = t // tm
    row = lambda i: (i, 0)
    const = lambda i: (0, 0)
    pos = lambda i: (i % n_pos_tiles, 0)
    outs = pl.pallas_call(
        functools.partial(_inproj_kernel, aw=aw, lw=lw, d=d, head_dim=head_dim, attn_aux=attn_aux),
        grid=(nt,),
        in_specs=[
            pl.BlockSpec((tm, d), row),
            pl.BlockSpec((1, d), const),
            pl.BlockSpec(w_bf.shape, const),
            pl.BlockSpec((1, aw), const),
            pl.BlockSpec((1, aw), const),
            pl.BlockSpec((tm, LANES), pos),
            pl.BlockSpec((tm, LANES), pos),
            pl.BlockSpec((aw, aw), const),
        ],
        out_specs=[
            pl.BlockSpec((tm, aw), row),
            pl.BlockSpec((tm, aw), row),
            pl.BlockSpec((tm, aw), row),
            pl.BlockSpec((tm, aw), row),
            pl.BlockSpec((1, aw, tm), lambda i: (i, 0, 0)),
            pl.BlockSpec((1, 1, aw), lambda i: (i, 0, 0)),
            pl.BlockSpec((tm, lw), row),
            pl.BlockSpec((tm, lw), row),
            pl.BlockSpec((tm, d), row),
            pl.BlockSpec((tm, d), row),
        ],
        out_shape=[
            jax.ShapeDtypeStruct((t, aw), F32),
            jax.ShapeDtypeStruct((t, aw), F32),
            jax.ShapeDtypeStruct((t, aw), F32),
            jax.ShapeDtypeStruct((t, aw), BF16),
            jax.ShapeDtypeStruct((nt, aw, tm), BF16),
            jax.ShapeDtypeStruct((nt, 1, aw), F32),
            jax.ShapeDtypeStruct((t, lw), F32),
            jax.ShapeDtypeStruct((t, lw), F32),
            jax.ShapeDtypeStruct((t, d), F32),
            jax.ShapeDtypeStruct((t, d), F32),
        ],
        compiler_params=_cparams(("arbitrary",)),
        name="inproj",
    )(x2d, g1, w_bf, qg, kg, cos, sin, bd)
    return outs


def _top_rows(g, n_rows, topk, valid_floor):
    row = lax.broadcasted_iota(I32, g.shape, 0)
    sel = jnp.zeros(g.shape, dtype=jnp.bool_)
    idxs = []
    vals = []
    for _ in range(topk):
        mx = jnp.max(g, axis=0, keepdims=True)
        idx = jnp.min(jnp.where(g == mx, row, n_rows), axis=0, keepdims=True)
        hit = row == idx
        sel = sel | (hit & (mx > valid_floor))
        g = jnp.where(hit, -jnp.inf, g)
        idxs.append(idx)
        vals.append(mx)
    return idxs, vals, sel


def _attn_kernel(q_ref, k_ref, vt_ref, km_ref, o_ref, sel_ref, *, nb, blk, head_dim, topk, n_pairs):
    i = pl.program_id(2)
    n_h = LANES // head_dim
    n_heads = n_pairs * n_h
    lane = lax.broadcasted_iota(I32, (blk, LANES), 1)
    brow = lax.broadcasted_iota(I32, (nb, blk), 0)
    qscale = head_dim ** -0.5 * LOG2E
    qs = []
    for pp in range(n_pairs):
        lanes = slice(pp * LANES, (pp + 1) * LANES)
        q = q_ref[0, :, lanes]
        km = km_ref[0, :, lanes].astype(BF16)
        parts = []
        for hh in range(n_h):
            in_head = (lane >= hh * head_dim) & (lane < (hh + 1) * head_dim)
            qm = jnp.where(in_head, q, 0.0)
            gate = lax.dot_general(km, qm.astype(BF16), NT, preferred_element_type=F32)
            gate = jnp.where(brow < i, gate, NEG)
            _, _, sel = _top_rows(gate, nb, topk, NEG)
            sel_ref[pp * n_h + hh] = sel.astype(F32)
            parts.append((qm * qscale).astype(BF16))
        qs.append(jnp.concatenate(parts, axis=0))

    def scores(j, pp):
        return lax.dot_general(k_ref[0, j, :, pp * LANES:(pp + 1) * LANES], qs[pp], NT,
                               preferred_element_type=F32)

    def update(j, pp, hh, s, m, l, acc):
        m_new = jnp.maximum(m, jnp.max(s, axis=0, keepdims=True))
        alpha = jnp.exp2(m - m_new)
        p = jnp.exp2(s - m_new)
        l = alpha * l + jnp.sum(p, axis=0, keepdims=True)
        lo = pp * LANES + hh * head_dim
        pv = jnp.dot(vt_ref[0, j, lo:lo + head_dim, :], p.astype(BF16), preferred_element_type=F32)
        return m_new, l, alpha * acc + pv

    causal = (lax.broadcasted_iota(I32, (blk, blk), 0) <= lax.broadcasted_iota(I32, (blk, blk), 1))
    state = []
    for pp in range(n_pairs):
        s_own = scores(i, pp)
        for hh in range(n_h):
            s = jnp.where(causal, s_own[:, hh * blk:(hh + 1) * blk], NEG)
            state.extend(update(i, pp, hh, s, jnp.full((1, blk), NEG, F32),
                                jnp.zeros((1, blk), F32), jnp.zeros((head_dim, blk), F32)))

    def body(j, carry):
        out = []
        for pp in range(n_pairs):
            sj = scores(j, pp)
            for hh in range(n_h):
                n = pp * n_h + hh
                m, l, acc = carry[3 * n:3 * n + 3]
                s = jnp.where(sel_ref[n, pl.ds(j, 1), :] > 0.5, sj[:, hh * blk:(hh + 1) * blk], NEG)
                out.extend(update(j, pp, hh, s, m, l, acc))
        return tuple(out)

    final = lax.fori_loop(0, i, body, tuple(state))
    outs = [final[3 * n + 2] / final[3 * n + 1] for n in range(n_heads)]
    o_ref[0] = jnp.concatenate(outs, axis=0).T.astype(BF16)


def _attention(q3, k4, vt4, km3, *, head_dim, topk):
    b, s, aw = q3.shape
    nb, blk = k4.shape[1], k4.shape[2]
    n_pairs = 2 if aw % (2 * LANES) == 0 else 1
    gl = n_pairs * LANES
    return pl.pallas_call(
        functools.partial(_attn_kernel, nb=nb, blk=blk, head_dim=head_dim, topk=topk,
                          n_pairs=n_pairs),
        grid=(b, aw // gl, nb),
        in_specs=[
            pl.BlockSpec((1, blk, gl), lambda bi, hp, i: (bi, i, hp)),
            pl.BlockSpec((1, nb, blk, gl), lambda bi, hp, i: (bi, 0, 0, hp)),
            pl.BlockSpec((1, nb, gl, blk), lambda bi, hp, i: (bi, 0, hp, 0)),
            pl.BlockSpec((1, nb, gl), lambda bi, hp, i: (bi, 0, hp)),
        ],
        out_specs=pl.BlockSpec((1, blk, gl), lambda bi, hp, i: (bi, i, hp)),
        out_shape=jax.ShapeDtypeStruct((b, s, aw), BF16),
        scratch_shapes=[pltpu.VMEM((gl // head_dim, nb, blk), F32)],
        compiler_params=_cparams(("arbitrary", "arbitrary", "arbitrary")),
        name="moba_attn",
    )(q3, k4, vt4, km3)


def _col_bcast(row):
    return jnp.broadcast_to(row, (LANES, row.shape[1])).T


def _top_lanes(g, n_cols, topk):
    col = lax.broadcasted_iota(I32, g.shape, 1)
    idxs = []
    for _ in range(topk):
        mx = jnp.max(g, axis=1, keepdims=True)
        idx = jnp.min(jnp.where(g == mx, col, n_cols), axis=1, keepdims=True)
        g = jnp.where(col == idx, -jnp.inf, g)
        idxs.append(idx)
    return idxs


def _dec_gate_kernel(pt_ref, q_ref, *refs, n_pp, ppb, n_groups, blk, topk, head_dim):
    page_refs = refs[:n_pp]
    sel_ref, qcol_sc, g_sc = refs[n_pp:]
    g = pl.program_id(1)
    n_heads = qcol_sc.shape[0]
    gw = g_sc.shape[1]
    per_step = n_pp // ppb

    @pl.when(g == 0)
    def _():
        qc = _col_bcast(q_ref[0])
        for h in range(n_heads):
            qcol_sc[h] = qc[h * head_dim:(h + 1) * head_dim]
        g_sc[...] = jnp.full(g_sc.shape, NEG, F32)

    lane = lax.broadcasted_iota(I32, (n_heads, gw), 1)
    gates = g_sc[...]
    qcol = qcol_sc[:, :, 0:1].astype(BF16).astype(F32)
    for u in range(per_step):
        tot = page_refs[u * ppb][0]
        for w in range(1, ppb):
            tot = tot + page_refs[u * ppb + w][0]
        k_mean = jnp.sum(tot, axis=2, keepdims=True) * (1.0 / blk)
        score = jnp.sum(k_mean.astype(BF16).astype(F32) * qcol, axis=1)
        gates = jnp.where(lane == g * per_step + u, score, gates)
    g_sc[...] = gates

    @pl.when(g == n_groups - 1)
    def _():
        idxs = _top_lanes(g_sc[...], gw, topk)
        olane = lax.broadcasted_iota(I32, (n_heads, LANES), 1)
        out = jnp.zeros((n_heads, LANES), I32)
        for r, idx in enumerate(idxs):
            out = jnp.where(olane == r, idx, out)
        sel_ref[0] = out


def _dec_select(page_table, q3, cache_kt, *, ppb, blk, topk):
    nbd, n_pages = page_table.shape
    _, n_heads, head_dim, page = cache_kt.shape
    aw = n_heads * head_dim
    n_pp = ppb
    while n_pp * 2 <= 16 and n_pages % (n_pp * 2) == 0:
        n_pp *= 2
    n_groups = n_pages // n_pp
    nbp = n_pages // ppb
    gw = -(-nbp // LANES) * LANES

    def page_spec(u):
        return pl.BlockSpec((1, n_heads, head_dim, page),
                            lambda bi, g, pt: (pt[bi, g * n_pp + u], 0, 0, 0))

    grid_spec = pltpu.PrefetchScalarGridSpec(
        num_scalar_prefetch=1,
        grid=(nbd, n_groups),
        in_specs=[pl.BlockSpec((1, 1, aw), lambda bi, g, pt: (bi, 0, 0))]
        + [page_spec(u) for u in range(n_pp)],
        out_specs=pl.BlockSpec((1, n_heads, LANES), lambda bi, g, pt: (bi, 0, 0)),
        scratch_shapes=[pltpu.VMEM((n_heads, head_dim, LANES), F32),
                        pltpu.VMEM((n_heads, gw), F32)],
    )
    return pl.pallas_call(
        functools.partial(_dec_gate_kernel, n_pp=n_pp, ppb=ppb, n_groups=n_groups, blk=blk,
                          topk=topk, head_dim=head_dim),
        grid_spec=grid_spec,
        out_shape=jax.ShapeDtypeStruct((nbd, n_heads, LANES), I32),
        compiler_params=_cparams(("arbitrary", "arbitrary")),
        name="dec_block_select",
    )(page_table, q3, *([cache_kt] * n_pp))


def _dec_attn_kernel(pt_ref, sel_ref, q_ref, kn_ref, vn_ref, *refs, n_heads, head_dim, ppb, topk):
    n_pg = n_heads * ppb
    k_refs = refs[:n_pg]
    v_refs = refs[n_pg:2 * n_pg]
    o_ref, qcol_sc, m_sc, l_sc, acc_sc = refs[2 * n_pg:]
    r = pl.program_id(1)
    scale = head_dim ** -0.5
    lane = lax.broadcasted_iota(I32, (head_dim, LANES), 1)

    @pl.when(r == 0)
    def _():
        qc = _col_bcast(q_ref[0])
        kc = _col_bcast(kn_ref[0])
        vc = _col_bcast(vn_ref[0])
        for h in range(n_heads):
            rows = slice(h * head_dim, (h + 1) * head_dim)
            qcol_sc[h] = qc[rows]
            m_sc[h:h + 1, :] = jnp.sum(qc[rows] * kc[rows], axis=0, keepdims=True) * scale
            l_sc[h:h + 1, :] = jnp.ones((1, LANES), F32)
            acc_sc[h] = jnp.where(lane == 0, vc[rows], 0.0)

    for h in range(n_heads):
        qc = qcol_sc[h]
        m = m_sc[h:h + 1, :]
        ss = [jnp.sum(k_refs[h * ppb + w][0, 0] * qc, axis=0, keepdims=True) * scale
              for w in range(ppb)]
        blk_max = ss[0]
        for s in ss[1:]:
            blk_max = jnp.maximum(blk_max, s)
        m_new = jnp.maximum(m, jnp.max(blk_max, axis=1, keepdims=True))
        alpha = jnp.exp(m - m_new)
        l = alpha * l_sc[h:h + 1, :]
        acc = alpha * acc_sc[h]
        for w in range(ppb):
            p = jnp.exp(ss[w] - m_new)
            l = l + jnp.sum(p, axis=1, keepdims=True)
            acc = acc + v_refs[h * ppb + w][0, 0] * p
        m_sc[h:h + 1, :] = m_new
        l_sc[h:h + 1, :] = l
        acc_sc[h] = acc

    @pl.when(r == topk - 1)
    def _():
        cols = [jnp.broadcast_to(
            jnp.sum(acc_sc[h], axis=1, keepdims=True) / l_sc[h:h + 1, 0:1], (head_dim, LANES))
            for h in range(n_heads)]
        o_ref[0] = jnp.concatenate(cols, axis=0).T[0:1]


def _dec_attention(page_table, sel, q3, kn3, vn3, cache_kt, cache_vt, *, ppb, topk):
    nbd = page_table.shape[0]
    _, n_heads, head_dim, page = cache_kt.shape
    aw = n_heads * head_dim

    def page_spec(h, w):
        return pl.BlockSpec(
            (1, 1, head_dim, page),
            lambda bi, r, pt, sl: (pt[bi, ppb * sl[bi, h, r] + w], h, 0, 0))

    vec = pl.BlockSpec((1, 1, aw), lambda bi, r, pt, sl: (bi, 0, 0))
    pages = [page_spec(h, w) for h in range(n_heads) for w in range(ppb)]
    grid_spec = pltpu.PrefetchScalarGridSpec(
        num_scalar_prefetch=2,
        grid=(nbd, topk),
        in_specs=[vec, vec, vec] + pages + pages,
        out_specs=vec,
        scratch_shapes=[pltpu.VMEM((n_heads, head_dim, LANES), F32),
                        pltpu.VMEM((n_heads, LANES), F32),
                        pltpu.VMEM((n_heads, LANES), F32),
                        pltpu.VMEM((n_heads, head_dim, LANES), F32)],
    )
    n_pg = n_heads * ppb
    return pl.pallas_call(
        functools.partial(_dec_attn_kernel, n_heads=n_heads, head_dim=head_dim, ppb=ppb, topk=topk),
        grid_spec=grid_spec,
        out_shape=jax.ShapeDtypeStruct((nbd, 1, aw), F32),
        compiler_params=_cparams(("arbitrary", "arbitrary")),
        name="dec_attn",
    )(page_table, sel, q3, kn3, vn3, *([cache_kt] * n_pg), *([cache_vt] * n_pg))


def _lru_gates(xc, wrg_ref, brg, wig_ref, big, lam, first_pos_mask):
    nblk, bw = wrg_ref.shape[0], wrg_ref.shape[1]
    xcb = xc.astype(BF16)
    rz = jnp.concatenate(
        [jnp.dot(xcb[:, n * bw:(n + 1) * bw], wrg_ref[n], preferred_element_type=F32)
         for n in range(nblk)], axis=1)
    iz = jnp.concatenate(
        [jnp.dot(xcb[:, n * bw:(n + 1) * bw], wig_ref[n], preferred_element_type=F32)
         for n in range(nblk)], axis=1)
    r = jax.nn.sigmoid(rz + brg)
    ig = jax.nn.sigmoid(iz + big)
    neg_lam = -lam
    softplus = jnp.maximum(neg_lam, 0.0) + jnp.log1p(jnp.exp(-jnp.abs(neg_lam)))
    log_a = -LRU_C * r * softplus
    a = jnp.exp(log_a)
    mult = jnp.sqrt(jnp.tanh(-log_a) * (a * a + 1.0))
    if first_pos_mask is not None:
        mult = jnp.where(first_pos_mask, 1.0, mult)
    return a, xc * ig * mult


def _lru_kernel(xb_ref, gy_ref, cs_ref, h0_ref, cw_ref, cb_ref, wrg_ref, brg_ref, wig_ref, big_ref,
                lam_ref, out_ref, hlast_ref, xext, a_sc, u_sc, hcar, *, ts, pos0):
    t = pl.program_id(1)
    w = xb_ref.shape[1]
    halo = SUBLANES
    n_tap = cw_ref.shape[0]

    @pl.when(t == 0)
    def _():
        xext[0:halo, :] = jnp.zeros((halo, w), F32)
        xext[halo - (n_tap - 1):halo, :] = cs_ref[0]
        hcar[...] = h0_ref[0]

    xext[halo:halo + ts, :] = xb_ref[...]
    xc = cb_ref[...]
    for j in range(n_tap):
        lo = halo - (n_tap - 1) + j
        xc = xc + xext[lo:lo + ts, :] * cw_ref[j:j + 1, :]
    xext[0:halo, :] = xext[ts:ts + halo, :]

    row = lax.broadcasted_iota(I32, (ts, w), 0)
    first = (row + t * ts + pos0) == 0
    a, u = _lru_gates(xc, wrg_ref, brg_ref[...], wig_ref, big_ref[...], lam_ref[...], first)
    a_sc[...] = a
    u_sc[...] = u

    sub = lax.broadcasted_iota(I32, (SUBLANES, LANES), 0)
    carry = hcar[...]
    ends = []
    for lc in range(w // LANES):
        cols = slice(lc * LANES, (lc + 1) * LANES)
        state = jnp.broadcast_to(carry[:, cols], (SUBLANES, LANES))
        for g2 in range(ts // (2 * SUBLANES)):
            pair = []
            for g in (2 * g2, 2 * g2 + 1):
                rows = slice(g * SUBLANES, (g + 1) * SUBLANES)
                av = a_sc[rows, cols]
                uv = u_sc[rows, cols]
                for dist in (1, 2, 4):
                    a_up = jnp.where(sub >= dist, pltpu.roll(av, dist, 0), 1.0)
                    u_up = jnp.where(sub >= dist, pltpu.roll(uv, dist, 0), 0.0)
                    uv = uv + av * u_up
                    av = av * a_up
                h = uv + av * state
                state = jnp.broadcast_to(h[SUBLANES - 1:SUBLANES, :], (SUBLANES, LANES))
                pair.append(h)
            rows2 = slice(2 * g2 * SUBLANES, (2 * g2 + 2) * SUBLANES)
            out_ref[rows2, cols] = (jnp.concatenate(pair, axis=0) * gy_ref[rows2, cols]).astype(BF16)
        ends.append(state[0:1, :])
    carry = jnp.concatenate(ends, axis=1)
    hcar[...] = carry
    hlast_ref[0] = carry


def _lru_prompt(xb, gy, cs, h0, cw, cb, wrg, brg, wig, big, lam, *, b, s, ts, pos0):
    w = xb.shape[1]
    nt = s // ts
    assert ts % (2 * SUBLANES) == 0
    n_tap = cw.shape[0]
    row = lambda bi, t: (bi * nt + t, 0)
    const2 = lambda bi, t: (0, 0)
    const3 = lambda bi, t: (0, 0, 0)
    per_b = lambda bi, t: (bi, 0, 0)
    return pl.pallas_call(
        functools.partial(_lru_kernel, ts=ts, pos0=pos0),
        grid=(b, nt),
        in_specs=[
            pl.BlockSpec((ts, w), row),
            pl.BlockSpec((ts, w), row),
            pl.BlockSpec((1, n_tap - 1, w), per_b),
            pl.BlockSpec((1, 1, w), per_b),
            pl.BlockSpec(cw.shape, const2),
            pl.BlockSpec((1, w), const2),
            pl.BlockSpec(wrg.shape, const3),
            pl.BlockSpec((1, w), const2),
            pl.BlockSpec(wig.shape, const3),
            pl.BlockSpec((1, w), const2),
            pl.BlockSpec((1, w), const2),
        ],
        out_specs=[pl.BlockSpec((ts, w), row), pl.BlockSpec((1, 1, w), per_b)],
        out_shape=[jax.ShapeDtypeStruct((b * s, w), BF16), jax.ShapeDtypeStruct((b, 1, w), F32)],
        scratch_shapes=[
            pltpu.VMEM((ts + SUBLANES, w), F32),
            pltpu.VMEM((ts, w), F32),
            pltpu.VMEM((ts, w), F32),
            pltpu.VMEM((1, w), F32),
        ],
        compiler_params=_cparams(("arbitrary", "arbitrary")),
        name="rglru_scan",
    )(xb, gy, cs, h0, cw, cb, wrg, brg, wig, big, lam)


def _lru_step_kernel(xb_ref, gy_ref, cs_ref, h0_ref, cw_ref, cb_ref, wrg_ref, brg_ref, wig_ref,
                     big_ref, lam_ref, out_ref, h_ref):
    n_tap = cw_ref.shape[0]
    xc = cb_ref[...]
    for j in range(n_tap - 1):
        xc = xc + cs_ref[j] * cw_ref[j:j + 1, :]
    xc = xc + xb_ref[...] * cw_ref[n_tap - 1:n_tap, :]
    a, u = _lru_gates(xc, wrg_ref, brg_ref[...], wig_ref, big_ref[...], lam_ref[...], None)
    h = a * h0_ref[...] + u
    h_ref[...] = h
    out_ref[...] = h * gy_ref[...]


def _lru_step(xb, gy, cs_t, h0, cw, cb, wrg, brg, wig, big, lam):
    n, w = xb.shape
    return pl.pallas_call(
        _lru_step_kernel,
        out_shape=[jax.ShapeDtypeStruct((n, w), F32), jax.ShapeDtypeStruct((n, w), F32)],
        name="rglru_step",
    )(xb, gy, cs_t, h0, cw, cb, wrg, brg, wig, big, lam)


def _merge_kernel(attn_ref, lru_ref, sga_ref, sgl_ref, x_ref, wao_ref, wlo_ref, wout_ref, g2_ref,
                  wrt_ref, br_ref, h_ref, hn_ref, te_ref, gw_ref, rk_ref, cnt_ref, carry_sc,
                  *, n_exp, topk):
    i = pl.program_id(0)
    tm = x_ref.shape[0]

    @pl.when(i == 0)
    def _():
        carry_sc[...] = jnp.zeros(carry_sc.shape, F32)

    att = jnp.dot(attn_ref[...].astype(BF16), wao_ref[...], preferred_element_type=F32)
    lru = jnp.dot(lru_ref[...].astype(BF16), wlo_ref[...], preferred_element_type=F32)
    mixed = sga_ref[...] * att + sgl_ref[...] * lru
    h = x_ref[...] + jnp.dot(mixed.astype(BF16), wout_ref[...], preferred_element_type=F32)
    h_ref[...] = h
    hn = h * lax.rsqrt(jnp.mean(h * h, axis=-1, keepdims=True) + NORM_EPS) * g2_ref[...]
    hn_ref[...] = hn

    logits = lax.dot_general(wrt_ref[...], hn.astype(BF16), NT,
                             preferred_element_type=F32) + br_ref[...]
    idxs, vals, sel = _top_rows(logits, n_exp, topk, -jnp.inf)
    top_v = jnp.concatenate(vals, axis=0)
    ex = jnp.exp(top_v - top_v[0:1])
    gw_ref[...] = ex / jnp.sum(ex, axis=0, keepdims=True)
    te_ref[...] = jnp.concatenate(idxs, axis=0)

    before = lax.broadcasted_iota(I32, (tm, tm), 0) < lax.broadcasted_iota(I32, (tm, tm), 1)
    sel_bf = sel.astype(BF16)
    prior = jnp.dot(sel_bf, before.astype(BF16), preferred_element_type=F32)
    base = prior + carry_sc[:, 0:1]
    erow = lax.broadcasted_iota(I32, (n_exp, tm), 0)
    ranks = [jnp.sum(jnp.where(erow == idx, base, 0.0), axis=0, keepdims=True) for idx in idxs]
    rk_ref[...] = jnp.concatenate(ranks, axis=0).astype(I32)
    carry_sc[...] = carry_sc[...] + jnp.sum(sel.astype(F32), axis=1, keepdims=True)
    cnt_ref[...] = carry_sc[...]


def _merge(attn, lru, sga, sgl, x2d, wao, wlo, wout, g2, wrt, br, *, tm, n_exp, topk):
    t, d = x2d.shape
    aw, lw = attn.shape[1], lru.shape[1]
    row = lambda i: (i, 0)
    col = lambda i: (0, i)
    const = lambda i: (0, 0)
    return pl.pallas_call(
        functools.partial(_merge_kernel, n_exp=n_exp, topk=topk),
        grid=(t // tm,),
        in_specs=[
            pl.BlockSpec((tm, aw), row),
            pl.BlockSpec((tm, lw), row),
            pl.BlockSpec((tm, d), row),
            pl.BlockSpec((tm, d), row),
            pl.BlockSpec((tm, d), row),
            pl.BlockSpec((aw, d), const),
            pl.BlockSpec((lw, d), const),
            pl.BlockSpec((d, d), const),
            pl.BlockSpec((1, d), const),
            pl.BlockSpec((n_exp, d), const),
            pl.BlockSpec((n_exp, 1), const),
        ],
        out_specs=[
            pl.BlockSpec((tm, d), row),
            pl.BlockSpec((tm, d), row),
            pl.BlockSpec((topk, tm), col),
            pl.BlockSpec((topk, tm), col),
            pl.BlockSpec((topk, tm), col),
            pl.BlockSpec((n_exp, LANES), const),
        ],
        out_shape=[
            jax.ShapeDtypeStruct((t, d), F32),
            jax.ShapeDtypeStruct((t, d), F32),
            jax.ShapeDtypeStruct((topk, t), I32),
            jax.ShapeDtypeStruct((topk, t), F32),
            jax.ShapeDtypeStruct((topk, t), I32),
            jax.ShapeDtypeStruct((n_exp, LANES), F32),
        ],
        scratch_shapes=[pltpu.VMEM((n_exp, LANES), F32)],
        compiler_params=_cparams(("arbitrary",)),
        name="merge_router",
    )(attn, lru, sga, sgl, x2d, wao, wlo, wout, g2, wrt, br)


def _row_copy(src, src_row, dst, dst_row, sem):
    return pltpu.make_async_copy(src.at[pl.ds(src_row, 1)], dst.at[pl.ds(dst_row, 1)], sem)


def _dispatch_kernel(dest_ref, hn_ref, buf_in, buf_out, sem, *, topk):
    del buf_in
    tm = hn_ref.shape[0]

    def issue(t, c):
        for k in range(topk):
            _row_copy(hn_ref, t, buf_out, dest_ref[k, t], sem).start()
        return c

    lax.fori_loop(0, tm, issue, 0, unroll=DMA_UNROLL)

    def drain(t, c):
        for k in range(topk):
            _row_copy(hn_ref, 0, buf_out, 0, sem).wait()
        return c

    lax.fori_loop(0, tm, drain, 0, unroll=DMA_UNROLL)


def _dispatch(dest, hn, n_rows, *, tm, topk):
    t, d = hn.shape
    buf0 = jnp.zeros((n_rows, d), hn.dtype)
    return pl.pallas_call(
        functools.partial(_dispatch_kernel, topk=topk),
        grid=(t // tm,),
        in_specs=[
            pl.BlockSpec((topk, tm), lambda i: (0, i), memory_space=pltpu.SMEM),
            pl.BlockSpec((tm, d), lambda i: (i, 0)),
            pl.BlockSpec(memory_space=pl.ANY),
        ],
        out_specs=pl.BlockSpec(memory_space=pl.ANY),
        out_shape=jax.ShapeDtypeStruct((n_rows, d), hn.dtype),
        scratch_shapes=[pltpu.SemaphoreType.DMA(())],
        input_output_aliases={2: 0},
        compiler_params=_cparams(("arbitrary",)),
        name="moe_dispatch",
    )(dest, hn, buf0)


def _expert_kernel(be_ref, nu_ref, x_ref, wg_ref, wl_ref, wd_ref, bg_ref, bl_ref, bd_ref, o_ref):
    i = pl.program_id(0)

    @pl.when(i < nu_ref[0])
    def _():
        x = x_ref[...].astype(BF16)
        g = jnp.dot(x, wg_ref[0], preferred_element_type=F32) + bg_ref[0]
        lin = jnp.dot(x, wl_ref[0], preferred_element_type=F32) + bl_ref[0]
        glu = jnp.minimum(g, SWIGLU_LIMIT)
        lin = jnp.clip(lin, -SWIGLU_LIMIT, SWIGLU_LIMIT)
        act = glu * jax.nn.sigmoid(SWIGLU_ALPHA * glu) * (lin + 1.0)
        o_ref[...] = jnp.dot(act.astype(BF16), wd_ref[0], preferred_element_type=F32) + bd_ref[0]

    @pl.when(i >= nu_ref[0])
    def _():
        o_ref[...] = jnp.zeros(o_ref.shape, F32)


def _experts(blk_e, n_used, buf, wg, wl, wd, bg, bl, bd, *, rows):
    n_rows, d = buf.shape
    de = wg.shape[2]
    xrow = lambda i, be, nu: (i, 0)
    wsel = lambda i, be, nu: (be[i], 0, 0)
    grid_spec = pltpu.PrefetchScalarGridSpec(
        num_scalar_prefetch=2,
        grid=(n_rows // rows,),
        in_specs=[
            pl.BlockSpec((rows, d), xrow),
            pl.BlockSpec((1, d, de), wsel),
            pl.BlockSpec((1, d, de), wsel),
            pl.BlockSpec((1, de, d), wsel),
            pl.BlockSpec((1, 1, de), wsel),
            pl.BlockSpec((1, 1, de), wsel),
            pl.BlockSpec((1, 1, d), wsel),
        ],
        out_specs=pl.BlockSpec((rows, d), xrow),
    )
    return pl.pallas_call(
        _expert_kernel,
        grid_spec=grid_spec,
        out_shape=jax.ShapeDtypeStruct((n_rows, d), F32),
        compiler_params=_cparams(("arbitrary",)),
        name="moe_experts",
    )(blk_e, n_used, buf, wg, wl, wd, bg, bl, bd)


def _combine_kernel(dest_ref, gw_ref, h_ref, obuf, y_ref, g_sc, sem, *, topk):
    tm = h_ref.shape[0]

    def issue(t, c):
        for k in range(topk):
            _row_copy(obuf, dest_ref[k, t], g_sc.at[k], t, sem).start()
        return c

    lax.fori_loop(0, tm, issue, 0, unroll=DMA_UNROLL)

    def drain(t, c):
        for k in range(topk):
            _row_copy(obuf, 0, g_sc.at[k], 0, sem).wait()
        return c

    lax.fori_loop(0, tm, drain, 0, unroll=DMA_UNROLL)
    gw = gw_ref[...]
    ff = g_sc[0] * gw[:, 0:1]
    for k in range(1, topk):
        ff = ff + g_sc[k] * gw[:, k:k + 1]
    y_ref[...] = h_ref[...] + ff


def _combine(dest, gw_t, h, obuf, *, tm, topk):
    t, d = h.shape
    return pl.pallas_call(
        functools.partial(_combine_kernel, topk=topk),
        grid=(t // tm,),
        in_specs=[
            pl.BlockSpec((topk, tm), lambda i: (0, i), memory_space=pltpu.SMEM),
            pl.BlockSpec((tm, topk), lambda i: (i, 0)),
            pl.BlockSpec((tm, d), lambda i: (i, 0)),
            pl.BlockSpec(memory_space=pl.ANY),
        ],
        out_specs=pl.BlockSpec((tm, d), lambda i: (i, 0)),
        out_shape=jax.ShapeDtypeStruct((t, d), F32),
        scratch_shapes=[pltpu.VMEM((topk, tm, d), F32), pltpu.SemaphoreType.DMA(())],
        compiler_params=_cparams(("arbitrary",)),
        name="moe_combine",
    )(dest, gw_t, h, obuf)


def _split_gu_kernel(w_ref, wg_ref, wl_ref, *, seg):
    x = w_ref[0].astype(BF16)
    src = lax.broadcasted_iota(I32, (2 * seg, 2 * seg), 0)
    dst = lax.broadcasted_iota(I32, (2 * seg, 2 * seg), 1)
    wanted = jnp.where(dst < seg, 2 * dst, 2 * (dst - seg) + 1)
    pick = (src == wanted).astype(BF16)
    for n in range(x.shape[1] // (2 * seg)):
        both = jnp.dot(x[:, n * 2 * seg:(n + 1) * 2 * seg], pick, preferred_element_type=F32)
        wg_ref[0, :, n * seg:(n + 1) * seg] = both[:, :seg].astype(BF16)
        wl_ref[0, :, n * seg:(n + 1) * seg] = both[:, seg:].astype(BF16)


def _split_gu(w_gu):
    n_exp, d, de2 = w_gu.shape
    rt = min(d, 512)
    seg = 256
    assert d % rt == 0 and de2 % (2 * seg) == 0
    return pl.pallas_call(
        functools.partial(_split_gu_kernel, seg=seg),
        grid=(n_exp, d // rt),
        in_specs=[pl.BlockSpec((1, rt, de2), lambda e, r: (e, r, 0))],
        out_specs=[pl.BlockSpec((1, rt, de2 // 2), lambda e, r: (e, r, 0))] * 2,
        out_shape=[jax.ShapeDtypeStruct((n_exp, d, de2 // 2), BF16)] * 2,
        compiler_params=_cparams(("arbitrary", "arbitrary")),
        name="split_gu",
    )(w_gu)


def _dest_kernel(ps_ref, te_ref, rk_ref, dest_ref, *, n_exp):
    te = te_ref[...]
    dest = rk_ref[...]
    for e in range(n_exp):
        dest = dest + jnp.where(te == e, ps_ref[e], 0)
    dest_ref[...] = dest


def _dest_rows(pad_start, te, rk):
    return pl.pallas_call(
        functools.partial(_dest_kernel, n_exp=pad_start.shape[0]),
        in_specs=[pl.BlockSpec(memory_space=pltpu.SMEM),
                  pl.BlockSpec(memory_space=pltpu.VMEM),
                  pl.BlockSpec(memory_space=pltpu.VMEM)],
        out_specs=pl.BlockSpec(memory_space=pltpu.VMEM),
        out_shape=jax.ShapeDtypeStruct(te.shape, I32),
        name="moe_dest",
    )(pad_start, te, rk)


def _moe(h, hn, te, gw, rk, cnt, wts, *, rows, tm_io, topk):
    t, d = h.shape
    wg, wl, wd, bg, bl, bd = wts
    n_exp = wg.shape[0]
    counts = cnt[:, 0].astype(I32)
    padded = (counts + rows - 1) // rows * rows
    pad_end = jnp.cumsum(padded)
    pad_start = pad_end - padded
    dest = _dest_rows(pad_start, te, rk)
    n_blk = -(-(t * topk) // rows) + n_exp
    blk_lo = jnp.arange(n_blk, dtype=I32) * rows
    blk_e = jnp.minimum(
        jnp.sum((pad_end[None, :] <= blk_lo[:, None]).astype(I32), axis=1), n_exp - 1)
    n_used = (pad_end[-1:] // rows).astype(I32)
    buf = _dispatch(dest, hn, n_blk * rows, tm=tm_io, topk=topk)
    obuf = _experts(blk_e, n_used, buf, wg, wl, wd, bg, bl, bd, rows=rows)
    return _combine(dest, gw.T, h, obuf, tm=tm_io, topk=topk)


def kernel(x_prompt, x_sample, cache_k, cache_v, state_h, state_conv, page_table, norm1_g, w_in,
           q_norm_g, k_norm_g, conv_w, conv_b, w_rg, b_rg, w_ig, b_ig, lru_lambda, w_attn_o,
           w_lru_o, w_out, norm2_g, w_router, b_router, w_gu, b_gu, w_dn, b_dn):
    depth = w_in.shape[0]
    assert depth == 1, "one layer per step is supported"
    bp, sp, d = x_prompt.shape
    nbd, sd, _ = x_sample.shape
    assert sd == 1, "decode handles one new token per sequence"
    page, n_heads, head_dim = cache_k.shape[2], cache_k.shape[3], cache_k.shape[4]
    aw = n_heads * head_dim
    lw = w_lru_o.shape[1]
    n_exp = w_router.shape[2]
    n_pages = page_table.shape[1]
    past_len = n_pages * page
    blk = MOBA_BLOCK
    assert sp % blk == 0 and past_len % blk == 0 and blk % page == 0
    assert LANES % head_dim == 0 and aw % LANES == 0 and page == LANES
    ppb = blk // page
    assert past_len // blk >= MOBA_TOPK

    w_in_bf = w_in[0].astype(BF16)
    wrg_bf, wig_bf = w_rg[0].astype(BF16), w_ig[0].astype(BF16)
    wao_bf, wlo_bf, wout_bf = w_attn_o[0].astype(BF16), w_lru_o[0].astype(BF16), w_out[0].astype(BF16)
    wg_bf, wl_bf = _split_gu(w_gu[0])
    wd_bf = w_dn[0].astype(BF16)
    bg = b_gu[0][:, None, 0::2]
    bl = b_gu[0][:, None, 1::2]
    bdn = b_dn[0][:, None, :]
    moe_w = (wg_bf, wl_bf, wd_bf, bg, bl, bdn)
    qg = jnp.tile(q_norm_g[0], n_heads).reshape(1, aw)
    kg = jnp.tile(k_norm_g[0], n_heads).reshape(1, aw)
    head_of = jnp.arange(aw, dtype=I32) // head_dim
    same_head = head_of[:, None] == head_of[None, :]
    bd_mean = (same_head.astype(F32) / head_dim).astype(BF16)
    g1 = norm1_g[0].reshape(1, d)
    g2 = norm2_g[0].reshape(1, d)
    wrt = w_router[0].T.astype(BF16)
    br = b_router[0].reshape(n_exp, 1)
    cw, cb = conv_w[0], conv_b[0].reshape(1, lw)
    brg, big, lam = b_rg[0].reshape(1, lw), b_ig[0].reshape(1, lw), lru_lambda[0].reshape(1, lw)
    n_tap = cw.shape[0]

    tp = bp * sp
    cos_p, sin_p = _rope_table(sp, 0, head_dim)
    xp2 = x_prompt.reshape(tp, d)
    (q_p, k_p, v_p, kbf_p, vt_p, km_p, xb_p, gy_p, sga_p, sgl_p) = _inproj(
        xp2, g1, w_in_bf, qg, kg, cos_p, sin_p, bd_mean,
        tm=blk, n_pos_tiles=sp // blk, aw=aw, lw=lw, head_dim=head_dim, attn_aux=True)
    nb = sp // blk
    attn_p = _attention(
        q_p.reshape(bp, sp, aw), kbf_p.reshape(bp, nb, blk, aw), vt_p.reshape(bp, nb, aw, blk),
        km_p.reshape(bp, nb, aw), head_dim=head_dim, topk=MOBA_TOPK).reshape(tp, aw)
    lru_p, hlast_p = _lru_prompt(
        xb_p, gy_p, jnp.zeros((bp, n_tap - 1, lw), F32), jnp.zeros((bp, 1, lw), F32),
        cw, cb, wrg_bf, brg, wig_bf, big, lam, b=bp, s=sp, ts=blk, pos0=0)
    h_p, hn_p, te_p, gw_p, rk_p, cnt_p = _merge(
        attn_p, lru_p, sga_p, sgl_p, xp2, wao_bf, wlo_bf, wout_bf, g2, wrt, br,
        tm=blk, n_exp=n_exp, topk=TOP_K)
    y_p = _moe(h_p, hn_p, te_p, gw_p, rk_p, cnt_p, moe_w, rows=512, tm_io=blk, topk=TOP_K)
    assert sp >= n_tap - 1
    conv_p = xb_p.reshape(bp, sp, lw)[:, sp - (n_tap - 1):]

    cos_d, sin_d = _rope_table(sd, past_len, head_dim)
    cos_d = jnp.tile(cos_d, (nbd, 1))
    sin_d = jnp.tile(sin_d, (nbd, 1))
    xd2 = x_sample.reshape(nbd, d)
    (q_d, k_d, v_d, _, _, _, xb_d, gy_d, sga_d, sgl_d) = _inproj(
        xd2, g1, w_in_bf, qg, kg, cos_d, sin_d, bd_mean,
        tm=nbd, n_pos_tiles=1, aw=aw, lw=lw, head_dim=head_dim, attn_aux=False)
    ckt = cache_k[0].transpose(0, 2, 3, 1)
    cvt = cache_v[0].transpose(0, 2, 3, 1)
    q_d3 = q_d.reshape(nbd, 1, aw)
    sel = _dec_select(page_table, q_d3, ckt, ppb=ppb, blk=blk, topk=MOBA_TOPK)[:, :, :MOBA_TOPK]
    attn_d = _dec_attention(
        page_table, sel, q_d3, k_d.reshape(nbd, 1, aw), v_d.reshape(nbd, 1, aw), ckt, cvt,
        ppb=ppb, topk=MOBA_TOPK).reshape(nbd, aw)
    cs_d = state_conv[0]
    lru_d, h_d = _lru_step(xb_d, gy_d, cs_d.transpose(1, 0, 2), state_h[0], cw, cb,
                           wrg_bf, brg, wig_bf, big, lam)
    hd_, hn_d, te_d, gw_d, rk_d, cnt_d = _merge(
        attn_d, lru_d, sga_d, sgl_d, xd2, wao_bf, wlo_bf, wout_bf, g2, wrt, br,
        tm=nbd, n_exp=n_exp, topk=TOP_K)
    y_d = _moe(hd_, hn_d, te_d, gw_d, rk_d, cnt_d, moe_w, rows=2 * SUBLANES, tm_io=nbd, topk=TOP_K)
    conv_d = jnp.concatenate([cs_d, xb_d.reshape(nbd, sd, lw)], axis=1)[:, sd:]

    return (
        y_p.reshape(bp, sp, d),
        y_d.reshape(nbd, sd, d),
        k_p.reshape(1, bp, sp, n_heads, head_dim),
        v_p.reshape(1, bp, sp, n_heads, head_dim),
        hlast_p.reshape(1, bp, lw),
        conv_p.reshape(1, bp, n_tap - 1, lw),
        k_d.reshape(1, nbd, sd, n_heads, head_dim),
        v_d.reshape(1, nbd, sd, n_heads, head_dim),
        h_d.reshape(1, nbd, lw),
        conv_d.reshape(1, nbd, n_tap - 1, lw),
    )
```

```python
import functools

import jax
import jax.numpy as jnp
from jax import lax
from jax.experimental import pallas as pl
from jax.experimental.pallas import tpu as pltpu

F32 = jnp.float32
BF16 = jnp.bfloat16
I32 = jnp.int32

MOBA_BLOCK = 256
MOBA_TOPK = 3
ROPE_THETA = 10000.0
LRU_C = 8.0
TOP_K = 4
SWIGLU_LIMIT = 7.0
SWIGLU_ALPHA = 1.702
NORM_EPS = 1e-6

LANES = 128
SUBLANES = 8
VMEM_LIMIT = 56 * 1024 * 1024
DMA_UNROLL = 8
NEG = float(jnp.finfo(jnp.float32).min)
LOG2E = 1.4426950408889634
HIGHEST = lax.Precision.HIGHEST
NT = (((1,), (1,)), ((), ()))


def _cparams(sem):
    return pltpu.CompilerParams(dimension_semantics=sem, vmem_limit_bytes=VMEM_LIMIT)


def _rope_table_kernel(inv_ref, cos_ref, sin_ref, *, pos0, rows, head_dim):
    i = pl.program_id(0)
    pos = lax.broadcasted_iota(I32, (rows, LANES), 0) + (pos0 + i * rows)
    lane = lax.broadcasted_iota(I32, (rows, LANES), 1)
    ang = pos.astype(F32) * inv_ref[...]
    first_half = (lane % head_dim) < (head_dim // 2)
    cos_ref[...] = jnp.cos(ang)
    s = jnp.sin(ang)
    sin_ref[...] = jnp.where(first_half, -s, s)


def _rope_table(n_pos, pos0, head_dim):
    half = head_dim // 2
    inv = ROPE_THETA ** (-jnp.arange(half, dtype=F32) * 2.0 / head_dim)
    inv_l = jnp.tile(inv, LANES // half).reshape(1, LANES)
    n_pad = -(-n_pos // SUBLANES) * SUBLANES
    rows = min(n_pad, 512)
    assert n_pad % rows == 0
    cos, sin = pl.pallas_call(
        functools.partial(_rope_table_kernel, pos0=pos0, rows=rows, head_dim=head_dim),
        grid=(n_pad // rows,),
        in_specs=[pl.BlockSpec((1, LANES), lambda i: (0, 0))],
        out_specs=[pl.BlockSpec((rows, LANES), lambda i: (i, 0))] * 2,
        out_shape=[jax.ShapeDtypeStruct((n_pad, LANES), F32)] * 2,
        name="rope_table",
    )(inv_l)
    return cos[:n_pos], sin[:n_pos]


def _inproj_kernel(x_ref, g1_ref, w_ref, qg_ref, kg_ref, cos_ref, sin_ref, bd_ref,
                   q_ref, k_ref, v_ref, kbf_ref, vt_ref, km_ref, xb_ref, gy_ref, sga_ref, sgl_ref,
                   *, aw, lw, d, head_dim, attn_aux):
    x = x_ref[...]
    xn = x * lax.rsqrt(jnp.mean(x * x, axis=-1, keepdims=True) + NORM_EPS) * g1_ref[...]
    xn = xn.astype(BF16)
    tm = x.shape[0]

    def proj(lo, width):
        return jnp.dot(xn, w_ref[:, lo:lo + width], preferred_element_type=F32)

    reps = aw // LANES
    cos = jnp.concatenate([cos_ref[...]] * reps, axis=1)
    sin = jnp.concatenate([sin_ref[...]] * reps, axis=1)
    lane = lax.broadcasted_iota(I32, (tm, aw), 1)
    first_half = (lane % head_dim) < (head_dim // 2)
    bd = bd_ref[...]

    def head_norm_rope(t, g):
        ss = t * t
        hi = ss.astype(BF16)
        lo = (ss - hi.astype(F32)).astype(BF16)
        ms = (jnp.dot(hi, bd, preferred_element_type=F32)
              + jnp.dot(lo, bd, preferred_element_type=F32))
        tn = t * lax.rsqrt(ms + NORM_EPS) * g
        up = pltpu.roll(tn, aw - head_dim // 2, 1)
        dn = pltpu.roll(tn, head_dim // 2, 1)
        return tn * cos + jnp.where(first_half, up, dn) * sin

    q_ref[...] = head_norm_rope(proj(0, aw), qg_ref[...])
    k = head_norm_rope(proj(aw, aw), kg_ref[...])
    k_ref[...] = k
    v = proj(2 * aw, aw)
    v_ref[...] = v
    if attn_aux:
        kbf_ref[...] = k.astype(BF16)
        km_ref[0] = jnp.mean(k, axis=0, keepdims=True)
        vt_ref[0] = v.T.astype(BF16)
    else:
        kbf_ref[...] = jnp.zeros(kbf_ref.shape, BF16)
        km_ref[...] = jnp.zeros(km_ref.shape, F32)
        vt_ref[...] = jnp.zeros(vt_ref.shape, BF16)
    xb_ref[...] = proj(3 * aw, lw)
    gy_ref[...] = jax.nn.gelu(proj(3 * aw + lw, lw))
    sga_ref[...] = jax.nn.sigmoid(proj(3 * aw + 2 * lw, d))
    sgl_ref[...] = jax.nn.sigmoid(proj(3 * aw + 2 * lw + d, d))


def _inproj(x2d, g1, w_bf, qg, kg, cos, sin, bd, *, tm, n_pos_tiles, aw, lw, head_dim, attn_aux):
    t, d = x2d.shape
    nt = t // tm
    row = lambda i: (i, 0)
    const = lambda i: (0, 0)
    pos = lambda i: (i % n_pos_tiles, 0)
    outs = pl.pallas_call(
        functools.partial(_inproj_kernel, aw=aw, lw=lw, d=d, head_dim=head_dim, attn_aux=attn_aux),
        grid=(nt,),
        in_specs=[
            pl.BlockSpec((tm, d), row),
            pl.BlockSpec((1, d), const),
            pl.BlockSpec(w_bf.shape, const),
            pl.BlockSpec((1, aw), const),
            pl.BlockSpec((1, aw), const),
            pl.BlockSpec((tm, LANES), pos),
            pl.BlockSpec((tm, LANES), pos),
            pl.BlockSpec((aw, aw), const),
        ],
        out_specs=[
            pl.BlockSpec((tm, aw), row),
            pl.BlockSpec((tm, aw), row),
            pl.BlockSpec((tm, aw), row),
            pl.BlockSpec((tm, aw), row),
            pl.BlockSpec((1, aw, tm), lambda i: (i, 0, 0)),
            pl.BlockSpec((1, 1, aw), lambda i: (i, 0, 0)),
            pl.BlockSpec((tm, lw), row),
            pl.BlockSpec((tm, lw), row),
            pl.BlockSpec((tm, d), row),
            pl.BlockSpec((tm, d), row),
        ],
        out_shape=[
            jax.ShapeDtypeStruct((t, aw), F32),
            jax.ShapeDtypeStruct((t, aw), F32),
            jax.ShapeDtypeStruct((t, aw), F32),
            jax.ShapeDtypeStruct((t, aw), BF16),
            jax.ShapeDtypeStruct((nt, aw, tm), BF16),
            jax.ShapeDtypeStruct((nt, 1, aw), F32),
            jax.ShapeDtypeStruct((t, lw), F32),
            jax.ShapeDtypeStruct((t, lw), F32),
            jax.ShapeDtypeStruct((t, d), F32),
            jax.ShapeDtypeStruct((t, d), F32),
        ],
        compiler_params=_cparams(("arbitrary",)),
        name="inproj",
    )(x2d, g1, w_bf, qg, kg, cos, sin, bd)
    return outs


def _top_rows(g, n_rows, topk, valid_floor):
    row = lax.broadcasted_iota(I32, g.shape, 0)
    sel = jnp.zeros(g.shape, dtype=jnp.bool_)
    idxs = []
    vals = []
    for _ in range(topk):
        mx = jnp.max(g, axis=0, keepdims=True)
        idx = jnp.min(jnp.where(g == mx, row, n_rows), axis=0, keepdims=True)
        hit = row == idx
        sel = sel | (hit & (mx > valid_floor))
        g = jnp.where(hit, -jnp.inf, g)
        idxs.append(idx)
        vals.append(mx)
    return idxs, vals, sel


def _attn_kernel(q_ref, k_ref, vt_ref, km_ref, o_ref, sel_ref, *, nb, blk, head_dim, topk, n_pairs):
    i = pl.program_id(2)
    n_h = LANES // head_dim
    n_heads = n_pairs * n_h
    lane = lax.broadcasted_iota(I32, (blk, LANES), 1)
    brow = lax.broadcasted_iota(I32, (nb, blk), 0)
    qscale = head_dim ** -0.5 * LOG2E
    qs = []
    for pp in range(n_pairs):
        lanes = slice(pp * LANES, (pp + 1) * LANES)
        q = q_ref[0, :, lanes]
        km = km_ref[0, :, lanes].astype(BF16)
        parts = []
        for hh in range(n_h):
            in_head = (lane >= hh * head_dim) & (lane < (hh + 1) * head_dim)
            qm = jnp.where(in_head, q, 0.0)
            gate = lax.dot_general(km, qm.astype(BF16), NT, preferred_element_type=F32)
            gate = jnp.where(brow < i, gate, NEG)
            _, _, sel = _top_rows(gate, nb, topk, NEG)
            sel_ref[pp * n_h + hh] = sel.astype(F32)
            parts.append((qm * qscale).astype(BF16))
        qs.append(jnp.concatenate(parts, axis=0))

    def scores(j, pp):
        return lax.dot_general(k_ref[0, j, :, pp * LANES:(pp + 1) * LANES], qs[pp], NT,
                               preferred_element_type=F32)

    def update(j, pp, hh, s, m, l, acc):
        m_new = jnp.maximum(m, jnp.max(s, axis=0, keepdims=True))
        alpha = jnp.exp2(m - m_new)
        p = jnp.exp2(s - m_new)
        l = alpha * l + jnp.sum(p, axis=0, keepdims=True)
        lo = pp * LANES + hh * head_dim
        pv = jnp.dot(vt_ref[0, j, lo:lo + head_dim, :], p.astype(BF16), preferred_element_type=F32)
        return m_new, l, alpha * acc + pv

    causal = (lax.broadcasted_iota(I32, (blk, blk), 0) <= lax.broadcasted_iota(I32, (blk, blk), 1))
    state = []
    for pp in range(n_pairs):
        s_own = scores(i, pp)
        for hh in range(n_h):
            s = jnp.where(causal, s_own[:, hh * blk:(hh + 1) * blk], NEG)
            state.extend(update(i, pp, hh, s, jnp.full((1, blk), NEG, F32),
                                jnp.zeros((1, blk), F32), jnp.zeros((head_dim, blk), F32)))

    def body(j, carry):
        s_cur, st = carry[:n_pairs], carry[n_pairs:]
        s_next = [scores(j + 1, pp) for pp in range(n_pairs)]
        out = []
        for pp in range(n_pairs):
            for hh in range(n_h):
                n = pp * n_h + hh
                m, l, acc = st[3 * n:3 * n + 3]
                s = jnp.where(sel_ref[n, pl.ds(j, 1), :] > 0.5,
                              s_cur[pp][:, hh * blk:(hh + 1) * blk], NEG)
                out.extend(update(j, pp, hh, s, m, l, acc))
        return tuple(s_next) + tuple(out)

    first = [scores(0, pp) for pp in range(n_pairs)]
    final = lax.fori_loop(0, i, body, tuple(first) + tuple(state))[n_pairs:]
    outs = [final[3 * n + 2] / final[3 * n + 1] for n in range(n_heads)]
    o_ref[0] = jnp.concatenate(outs, axis=0).T.astype(BF16)


def _attention(q3, k4, vt4, km3, *, head_dim, topk):
    b, s, aw = q3.shape
    nb, blk = k4.shape[1], k4.shape[2]
    n_pairs = 2 if aw % (2 * LANES) == 0 else 1
    gl = n_pairs * LANES
    return pl.pallas_call(
        functools.partial(_attn_kernel, nb=nb, blk=blk, head_dim=head_dim, topk=topk,
                          n_pairs=n_pairs),
        grid=(b, aw // gl, nb),
        in_specs=[
            pl.BlockSpec((1, blk, gl), lambda bi, hp, i: (bi, i, hp)),
            pl.BlockSpec((1, nb, blk, gl), lambda bi, hp, i: (bi, 0, 0, hp)),
            pl.BlockSpec((1, nb, gl, blk), lambda bi, hp, i: (bi, 0, hp, 0)),
            pl.BlockSpec((1, nb, gl), lambda bi, hp, i: (bi, 0, hp)),
        ],
        out_specs=pl.BlockSpec((1, blk, gl), lambda bi, hp, i: (bi, i, hp)),
        out_shape=jax.ShapeDtypeStruct((b, s, aw), BF16),
        scratch_shapes=[pltpu.VMEM((gl // head_dim, nb, blk), F32)],
        compiler_params=_cparams(("arbitrary", "arbitrary", "arbitrary")),
        name="moba_attn",
    )(q3, k4, vt4, km3)


def _col_bcast(row):
    return jnp.broadcast_to(row, (LANES, row.shape[1])).T


def _top_lanes(g, n_cols, topk):
    col = lax.broadcasted_iota(I32, g.shape, 1)
    idxs = []
    for _ in range(topk):
        mx = jnp.max(g, axis=1, keepdims=True)
        idx = jnp.min(jnp.where(g == mx, col, n_cols), axis=1, keepdims=True)
        g = jnp.where(col == idx, -jnp.inf, g)
        idxs.append(idx)
    return idxs


def _dec_gate_kernel(pt_ref, q_ref, *refs, n_pp, ppb, n_groups, blk, topk, head_dim):
    page_refs = refs[:n_pp]
    sel_ref, qcol_sc, g_sc = refs[n_pp:]
    g = pl.program_id(1)
    n_heads = qcol_sc.shape[0]
    gw = g_sc.shape[1]
    per_step = n_pp // ppb

    @pl.when(g == 0)
    def _():
        qc = _col_bcast(q_ref[0])
        for h in range(n_heads):
            qcol_sc[h] = qc[h * head_dim:(h + 1) * head_dim]
        g_sc[...] = jnp.full(g_sc.shape, NEG, F32)

    lane = lax.broadcasted_iota(I32, (n_heads, gw), 1)
    gates = g_sc[...]
    qcol = qcol_sc[...]
    for u in range(per_step):
        tot = (page_refs[u * ppb][0] * qcol).sum(axis=1)
        for w in range(1, ppb):
            tot = tot + (page_refs[u * ppb + w][0] * qcol).sum(axis=1)
        score = jnp.sum(tot, axis=1, keepdims=True) * (1.0 / blk)
        gates = jnp.where(lane == g * per_step + u, score, gates)
    g_sc[...] = gates

    @pl.when(g == n_groups - 1)
    def _():
        idxs = _top_lanes(g_sc[...], gw, topk)
        olane = lax.broadcasted_iota(I32, (n_heads, LANES), 1)
        out = jnp.zeros((n_heads, LANES), I32)
        for r, idx in enumerate(idxs):
            out = jnp.where(olane == r, idx, out)
        sel_ref[0] = out


def _dec_select(page_table, q3, cache_kt, *, ppb, blk, topk):
    nbd, n_pages = page_table.shape
    _, n_heads, head_dim, page = cache_kt.shape
    aw = n_heads * head_dim
    n_pp = ppb
    while n_pp * 2 <= 16 and n_pages % (n_pp * 2) == 0:
        n_pp *= 2
    n_groups = n_pages // n_pp
    nbp = n_pages // ppb
    gw = -(-nbp // LANES) * LANES

    def page_spec(u):
        return pl.BlockSpec((1, n_heads, head_dim, page),
                            lambda bi, g, pt: (pt[bi, g * n_pp + u], 0, 0, 0))

    grid_spec = pltpu.PrefetchScalarGridSpec(
        num_scalar_prefetch=1,
        grid=(nbd, n_groups),
        in_specs=[pl.BlockSpec((1, 1, aw), lambda bi, g, pt: (bi, 0, 0))]
        + [page_spec(u) for u in range(n_pp)],
        out_specs=pl.BlockSpec((1, n_heads, LANES), lambda bi, g, pt: (bi, 0, 0)),
        scratch_shapes=[pltpu.VMEM((n_heads, head_dim, LANES), F32),
                        pltpu.VMEM((n_heads, gw), F32)],
    )
    return pl.pallas_call(
        functools.partial(_dec_gate_kernel, n_pp=n_pp, ppb=ppb, n_groups=n_groups, blk=blk,
                          topk=topk, head_dim=head_dim),
        grid_spec=grid_spec,
        out_shape=jax.ShapeDtypeStruct((nbd, n_heads, LANES), I32),
        compiler_params=_cparams(("arbitrary", "arbitrary")),
        name="dec_block_select",
    )(page_table, q3, *([cache_kt] * n_pp))


def _dec_attn_kernel(pt_ref, sel_ref, q_ref, kn_ref, vn_ref, *refs, n_heads, head_dim, ppb, topk):
    n_pg = n_heads * ppb
    k_refs = refs[:n_pg]
    v_refs = refs[n_pg:2 * n_pg]
    o_ref, qcol_sc, m_sc, l_sc, acc_sc = refs[2 * n_pg:]
    r = pl.program_id(1)
    scale = head_dim ** -0.5
    lane = lax.broadcasted_iota(I32, (head_dim, LANES), 1)

    @pl.when(r == 0)
    def _():
        qc = _col_bcast(q_ref[0])
        kc = _col_bcast(kn_ref[0])
        vc = _col_bcast(vn_ref[0])
        for h in range(n_heads):
            rows = slice(h * head_dim, (h + 1) * head_dim)
            qcol_sc[h] = qc[rows]
            m_sc[h:h + 1, :] = jnp.sum(qc[rows] * kc[rows], axis=0, keepdims=True) * scale
            l_sc[h:h + 1, :] = jnp.ones((1, LANES), F32)
            acc_sc[h] = jnp.where(lane == 0, vc[rows], 0.0)

    for h in range(n_heads):
        qc = qcol_sc[h]
        m = m_sc[h:h + 1, :]
        ss = [jnp.sum(k_refs[h * ppb + w][0, 0] * qc, axis=0, keepdims=True) * scale
              for w in range(ppb)]
        blk_max = ss[0]
        for s in ss[1:]:
            blk_max = jnp.maximum(blk_max, s)
        m_new = jnp.maximum(m, jnp.max(blk_max, axis=1, keepdims=True))
        alpha = jnp.exp(m - m_new)
        l = alpha * l_sc[h:h + 1, :]
        acc = alpha * acc_sc[h]
        for w in range(ppb):
            p = jnp.exp(ss[w] - m_new)
            l = l + jnp.sum(p, axis=1, keepdims=True)
            acc = acc + v_refs[h * ppb + w][0, 0] * p
        m_sc[h:h + 1, :] = m_new
        l_sc[h:h + 1, :] = l
        acc_sc[h] = acc

    @pl.when(r == topk - 1)
    def _():
        cols = [jnp.broadcast_to(
            jnp.sum(acc_sc[h], axis=1, keepdims=True) / l_sc[h:h + 1, 0:1], (head_dim, LANES))
            for h in range(n_heads)]
        o_ref[0] = jnp.concatenate(cols, axis=0).T[0:1]


def _dec_attention(page_table, sel, q3, kn3, vn3, cache_kt, cache_vt, *, ppb, topk):
    nbd = page_table.shape[0]
    _, n_heads, head_dim, page = cache_kt.shape
    aw = n_heads * head_dim

    def page_spec(h, w):
        return pl.BlockSpec(
            (1, 1, head_dim, page),
            lambda bi, r, pt, sl: (pt[bi, ppb * sl[bi, h, r] + w], h, 0, 0))

    vec = pl.BlockSpec((1, 1, aw), lambda bi, r, pt, sl: (bi, 0, 0))
    pages = [page_spec(h, w) for h in range(n_heads) for w in range(ppb)]
    grid_spec = pltpu.PrefetchScalarGridSpec(
        num_scalar_prefetch=2,
        grid=(nbd, topk),
        in_specs=[vec, vec, vec] + pages + pages,
        out_specs=vec,
        scratch_shapes=[pltpu.VMEM((n_heads, head_dim, LANES), F32),
                        pltpu.VMEM((n_heads, LANES), F32),
                        pltpu.VMEM((n_heads, LANES), F32),
                        pltpu.VMEM((n_heads, head_dim, LANES), F32)],
    )
    n_pg = n_heads * ppb
    return pl.pallas_call(
        functools.partial(_dec_attn_kernel, n_heads=n_heads, head_dim=head_dim, ppb=ppb, topk=topk),
        grid_spec=grid_spec,
        out_shape=jax.ShapeDtypeStruct((nbd, 1, aw), F32),
        compiler_params=_cparams(("arbitrary", "arbitrary")),
        name="dec_attn",
    )(page_table, sel, q3, kn3, vn3, *([cache_kt] * n_pg), *([cache_vt] * n_pg))


def _lru_gates(xc, wrg_ref, brg, wig_ref, big, lam, first_pos_mask):
    nblk, bw = wrg_ref.shape[0], wrg_ref.shape[1]
    xcb = xc.astype(BF16)
    rz = jnp.concatenate(
        [jnp.dot(xcb[:, n * bw:(n + 1) * bw], wrg_ref[n], preferred_element_type=F32)
         for n in range(nblk)], axis=1)
    iz = jnp.concatenate(
        [jnp.dot(xcb[:, n * bw:(n + 1) * bw], wig_ref[n], preferred_element_type=F32)
         for n in range(nblk)], axis=1)
    r = jax.nn.sigmoid(rz + brg)
    ig = jax.nn.sigmoid(iz + big)
    neg_lam = -lam
    softplus = jnp.maximum(neg_lam, 0.0) + jnp.log1p(jnp.exp(-jnp.abs(neg_lam)))
    log_a = -LRU_C * r * softplus
    a = jnp.exp(log_a)
    mult = jnp.sqrt(jnp.tanh(-log_a) * (a * a + 1.0))
    if first_pos_mask is not None:
        mult = jnp.where(first_pos_mask, 1.0, mult)
    return a, xc * ig * mult


def _lru_kernel(xb_ref, gy_ref, cs_ref, h0_ref, cw_ref, cb_ref, wrg_ref, brg_ref, wig_ref, big_ref,
                lam_ref, out_ref, hlast_ref, xext, a_sc, u_sc, hcar, *, ts, pos0):
    t = pl.program_id(1)
    w = xb_ref.shape[1]
    halo = SUBLANES
    n_tap = cw_ref.shape[0]

    @pl.when(t == 0)
    def _():
        xext[0:halo, :] = jnp.zeros((halo, w), F32)
        xext[halo - (n_tap - 1):halo, :] = cs_ref[0]
        hcar[...] = h0_ref[0]

    xext[halo:halo + ts, :] = xb_ref[...]
    xc = cb_ref[...]
    for j in range(n_tap):
        lo = halo - (n_tap - 1) + j
        xc = xc + xext[lo:lo + ts, :] * cw_ref[j:j + 1, :]
    xext[0:halo, :] = xext[ts:ts + halo, :]

    row = lax.broadcasted_iota(I32, (ts, w), 0)
    first = (row + t * ts + pos0) == 0
    a, u = _lru_gates(xc, wrg_ref, brg_ref[...], wig_ref, big_ref[...], lam_ref[...], first)
    a_sc[...] = a
    u_sc[...] = u

    sub = lax.broadcasted_iota(I32, (SUBLANES, LANES), 0)
    carry = hcar[...]
    ends = []
    for lc in range(w // LANES):
        cols = slice(lc * LANES, (lc + 1) * LANES)
        state = jnp.broadcast_to(carry[:, cols], (SUBLANES, LANES))
        for g2 in range(ts // (2 * SUBLANES)):
            pair = []
            for g in (2 * g2, 2 * g2 + 1):
                rows = slice(g * SUBLANES, (g + 1) * SUBLANES)
                av = a_sc[rows, cols]
                uv = u_sc[rows, cols]
                for dist in (1, 2, 4):
                    a_up = jnp.where(sub >= dist, pltpu.roll(av, dist, 0), 1.0)
                    u_up = jnp.where(sub >= dist, pltpu.roll(uv, dist, 0), 0.0)
                    uv = uv + av * u_up
                    av = av * a_up
                h = uv + av * state
                state = jnp.broadcast_to(h[SUBLANES - 1:SUBLANES, :], (SUBLANES, LANES))
                pair.append(h)
            rows2 = slice(2 * g2 * SUBLANES, (2 * g2 + 2) * SUBLANES)
            out_ref[rows2, cols] = (jnp.concatenate(pair, axis=0) * gy_ref[rows2, cols]).astype(BF16)
        ends.append(state[0:1, :])
    carry = jnp.concatenate(ends, axis=1)
    hcar[...] = carry
    hlast_ref[0] = carry


def _lru_prompt(xb, gy, cs, h0, cw, cb, wrg, brg, wig, big, lam, *, b, s, ts, pos0):
    w = xb.shape[1]
    nt = s // ts
    assert ts % (2 * SUBLANES) == 0
    n_tap = cw.shape[0]
    row = lambda bi, t: (bi * nt + t, 0)
    const2 = lambda bi, t: (0, 0)
    const3 = lambda bi, t: (0, 0, 0)
    per_b = lambda bi, t: (bi, 0, 0)
    return pl.pallas_call(
        functools.partial(_lru_kernel, ts=ts, pos0=pos0),
        grid=(b, nt),
        in_specs=[
            pl.BlockSpec((ts, w), row),
            pl.BlockSpec((ts, w), row),
            pl.BlockSpec((1, n_tap - 1, w), per_b),
            pl.BlockSpec((1, 1, w), per_b),
            pl.BlockSpec(cw.shape, const2),
            pl.BlockSpec((1, w), const2),
            pl.BlockSpec(wrg.shape, const3),
            pl.BlockSpec((1, w), const2),
            pl.BlockSpec(wig.shape, const3),
            pl.BlockSpec((1, w), const2),
            pl.BlockSpec((1, w), const2),
        ],
        out_specs=[pl.BlockSpec((ts, w), row), pl.BlockSpec((1, 1, w), per_b)],
        out_shape=[jax.ShapeDtypeStruct((b * s, w), BF16), jax.ShapeDtypeStruct((b, 1, w), F32)],
        scratch_shapes=[
            pltpu.VMEM((ts + SUBLANES, w), F32),
            pltpu.VMEM((ts, w), F32),
            pltpu.VMEM((ts, w), F32),
            pltpu.VMEM((1, w), F32),
        ],
        compiler_params=_cparams(("arbitrary", "arbitrary")),
        name="rglru_scan",
    )(xb, gy, cs, h0, cw, cb, wrg, brg, wig, big, lam)


def _lru_step_kernel(xb_ref, gy_ref, cs_ref, h0_ref, cw_ref, cb_ref, wrg_ref, brg_ref, wig_ref,
                     big_ref, lam_ref, out_ref, h_ref):
    n_tap = cw_ref.shape[0]
    xc = cb_ref[...]
    for j in range(n_tap - 1):
        xc = xc + cs_ref[j] * cw_ref[j:j + 1, :]
    xc = xc + xb_ref[...] * cw_ref[n_tap - 1:n_tap, :]
    a, u = _lru_gates(xc, wrg_ref, brg_ref[...], wig_ref, big_ref[...], lam_ref[...], None)
    h = a * h0_ref[...] + u
    h_ref[...] = h
    out_ref[...] = h * gy_ref[...]


def _lru_step(xb, gy, cs_t, h0, cw, cb, wrg, brg, wig, big, lam):
    n, w = xb.shape
    return pl.pallas_call(
        _lru_step_kernel,
        out_shape=[jax.ShapeDtypeStruct((n, w), F32), jax.ShapeDtypeStruct((n, w), F32)],
        name="rglru_step",
    )(xb, gy, cs_t, h0, cw, cb, wrg, brg, wig, big, lam)


def _merge_kernel(attn_ref, lru_ref, sga_ref, sgl_ref, x_ref, wao_ref, wlo_ref, wout_ref, g2_ref,
                  wrt_ref, br_ref, h_ref, hn_ref, te_ref, gw_ref, rk_ref, cnt_ref, carry_sc,
                  *, n_exp, topk):
    i = pl.program_id(0)
    tm = x_ref.shape[0]

    @pl.when(i == 0)
    def _():
        carry_sc[...] = jnp.zeros(carry_sc.shape, F32)

    att = jnp.dot(attn_ref[...].astype(BF16), wao_ref[...], preferred_element_type=F32)
    lru = jnp.dot(lru_ref[...].astype(BF16), wlo_ref[...], preferred_element_type=F32)
    mixed = sga_ref[...] * att + sgl_ref[...] * lru
    h = x_ref[...] + jnp.dot(mixed.astype(BF16), wout_ref[...], preferred_element_type=F32)
    h_ref[...] = h
    hn = h * lax.rsqrt(jnp.mean(h * h, axis=-1, keepdims=True) + NORM_EPS) * g2_ref[...]
    hn_ref[...] = hn

    logits = lax.dot_general(wrt_ref[...], hn.astype(BF16), NT,
                             preferred_element_type=F32) + br_ref[...]
    idxs, vals, sel = _top_rows(logits, n_exp, topk, -jnp.inf)
    top_v = jnp.concatenate(vals, axis=0)
    ex = jnp.exp(top_v - top_v[0:1])
    gw_ref[...] = ex / jnp.sum(ex, axis=0, keepdims=True)
    te_ref[...] = jnp.concatenate(idxs, axis=0)

    before = lax.broadcasted_iota(I32, (tm, tm), 0) < lax.broadcasted_iota(I32, (tm, tm), 1)
    sel_bf = sel.astype(BF16)
    prior = jnp.dot(sel_bf, before.astype(BF16), preferred_element_type=F32)
    base = prior + carry_sc[:, 0:1]
    erow = lax.broadcasted_iota(I32, (n_exp, tm), 0)
    ranks = [jnp.sum(jnp.where(erow == idx, base, 0.0), axis=0, keepdims=True) for idx in idxs]
    rk_ref[...] = jnp.concatenate(ranks, axis=0).astype(I32)
    carry_sc[...] = carry_sc[...] + jnp.sum(sel.astype(F32), axis=1, keepdims=True)
    cnt_ref[...] = carry_sc[...]


def _merge(attn, lru, sga, sgl, x2d, wao, wlo, wout, g2, wrt, br, *, tm, n_exp, topk):
    t, d = x2d.shape
    aw, lw = attn.shape[1], lru.shape[1]
    row = lambda i: (i, 0)
    col = lambda i: (0, i)
    const = lambda i: (0, 0)
    return pl.pallas_call(
        functools.partial(_merge_kernel, n_exp=n_exp, topk=topk),
        grid=(t // tm,),
        in_specs=[
            pl.BlockSpec((tm, aw), row),
            pl.BlockSpec((tm, lw), row),
            pl.BlockSpec((tm, d), row),
            pl.BlockSpec((tm, d), row),
            pl.BlockSpec((tm, d), row),
            pl.BlockSpec((aw, d), const),
            pl.BlockSpec((lw, d), const),
            pl.BlockSpec((d, d), const),
            pl.BlockSpec((1, d), const),
            pl.BlockSpec((n_exp, d), const),
            pl.BlockSpec((n_exp, 1), const),
        ],
        out_specs=[
            pl.BlockSpec((tm, d), row),
            pl.BlockSpec((tm, d), row),
            pl.BlockSpec((topk, tm), col),
            pl.BlockSpec((topk, tm), col),
            pl.BlockSpec((topk, tm), col),
            pl.BlockSpec((n_exp, LANES), const),
        ],
        out_shape=[
            jax.ShapeDtypeStruct((t, d), F32),
            jax.ShapeDtypeStruct((t, d), F32),
            jax.ShapeDtypeStruct((topk, t), I32),
            jax.ShapeDtypeStruct((topk, t), F32),
            jax.ShapeDtypeStruct((topk, t), I32),
            jax.ShapeDtypeStruct((n_exp, LANES), F32),
        ],
        scratch_shapes=[pltpu.VMEM((n_exp, LANES), F32)],
        compiler_params=_cparams(("arbitrary",)),
        name="merge_router",
    )(attn, lru, sga, sgl, x2d, wao, wlo, wout, g2, wrt, br)


def _row_copy(src, src_row, dst, dst_row, sem):
    return pltpu.make_async_copy(src.at[pl.ds(src_row, 1)], dst.at[pl.ds(dst_row, 1)], sem)


def _fill_copies(fill_ref, nu_ref, hn_ref, buf_out, sem, *, rows, n_blk):
    tm = hn_ref.shape[0]
    n_exp = fill_ref.shape[1]
    piece = min(rows, tm)
    assert rows & (rows - 1) == 0 and rows // 2 <= tm and rows % piece == 0
    assert rows % SUBLANES == 0

    def run(method):
        def per_expert(e, c):
            end, n = fill_ref[0, e], fill_ref[1, e]
            for r in range(SUBLANES - 1):
                @pl.when(r < (n & (SUBLANES - 1)))
                def _():
                    getattr(_row_copy(hn_ref, 0, buf_out, end - n + r, sem), method)()

            for b in range(SUBLANES.bit_length() - 1, rows.bit_length() - 1):
                size = 1 << b

                @pl.when(((n >> b) & 1) == 1)
                def _():
                    at = pl.multiple_of(end - ((n >> (b + 1)) << (b + 1)) - size, SUBLANES)
                    getattr(pltpu.make_async_copy(
                        hn_ref.at[pl.ds(0, size)], buf_out.at[pl.ds(at, size)], sem), method)()
            return c

        lax.fori_loop(0, n_exp, per_expert, 0)

        def per_block(bi, c):
            for p in range(rows // piece):
                getattr(pltpu.make_async_copy(
                    hn_ref.at[pl.ds(0, piece)],
                    buf_out.at[pl.ds(pl.multiple_of(bi * rows + p * piece, SUBLANES), piece)],
                    sem), method)()
            return c

        lax.fori_loop(nu_ref[0], n_blk, per_block, 0)

    return functools.partial(run, "start"), functools.partial(run, "wait")


def _dispatch_kernel(fill_ref, nu_ref, dest_ref, hn_ref, buf_out, sem, fill_sem, *, topk, rows,
                     n_blk):
    tm = hn_ref.shape[0]
    start_fill, wait_fill = _fill_copies(fill_ref, nu_ref, hn_ref, buf_out, fill_sem,
                                         rows=rows, n_blk=n_blk)

    @pl.when(pl.program_id(0) == 0)
    def _():
        start_fill()

    def issue(t, c):
        for k in range(topk):
            _row_copy(hn_ref, t, buf_out, dest_ref[k, t], sem).start()
        return c

    lax.fori_loop(0, tm, issue, 0, unroll=DMA_UNROLL)

    def drain(t, c):
        for k in range(topk):
            _row_copy(hn_ref, 0, buf_out, 0, sem).wait()
        return c

    lax.fori_loop(0, tm, drain, 0, unroll=DMA_UNROLL)

    @pl.when(pl.program_id(0) == 0)
    def _():
        wait_fill()


def _dispatch(fill, n_used, dest, hn, *, rows, n_blk, tm, topk):
    t, d = hn.shape
    grid_spec = pltpu.PrefetchScalarGridSpec(
        num_scalar_prefetch=2,
        grid=(t // tm,),
        in_specs=[
            pl.BlockSpec((topk, tm), lambda i, fl, nu: (0, i), memory_space=pltpu.SMEM),
            pl.BlockSpec((tm, d), lambda i, fl, nu: (i, 0)),
        ],
        out_specs=pl.BlockSpec(memory_space=pl.ANY),
        scratch_shapes=[pltpu.SemaphoreType.DMA(()), pltpu.SemaphoreType.DMA(())],
    )
    return pl.pallas_call(
        functools.partial(_dispatch_kernel, topk=topk, rows=rows, n_blk=n_blk),
        grid_spec=grid_spec,
        out_shape=jax.ShapeDtypeStruct((n_blk * rows, d), hn.dtype),
        compiler_params=_cparams(("arbitrary",)),
        name="moe_dispatch",
    )(fill, n_used, dest, hn)


def _expert_kernel(be_ref, nu_ref, x_ref, wg_ref, wl_ref, wd_ref, bg_ref, bl_ref, bd_ref, o_ref):
    i = pl.program_id(0)

    @pl.when(i < nu_ref[0])
    def _():
        x = x_ref[...].astype(BF16)
        g = jnp.dot(x, wg_ref[0], preferred_element_type=F32) + bg_ref[0]
        lin = jnp.dot(x, wl_ref[0], preferred_element_type=F32) + bl_ref[0]
        glu = jnp.minimum(g, SWIGLU_LIMIT)
        lin = jnp.clip(lin, -SWIGLU_LIMIT, SWIGLU_LIMIT)
        act = glu * jax.nn.sigmoid(SWIGLU_ALPHA * glu) * (lin + 1.0)
        o_ref[...] = jnp.dot(act.astype(BF16), wd_ref[0], preferred_element_type=F32) + bd_ref[0]

    @pl.when(i >= nu_ref[0])
    def _():
        o_ref[...] = jnp.zeros(o_ref.shape, F32)


def _experts(blk_e, n_used, buf, wg, wl, wd, bg, bl, bd, *, rows):
    n_rows, d = buf.shape
    de = wg.shape[2]
    xrow = lambda i, be, nu: (i, 0)
    xin = lambda i, be, nu: (jnp.minimum(i, nu[0] - 1), 0)
    wsel = lambda i, be, nu: (be[i], 0, 0)
    grid_spec = pltpu.PrefetchScalarGridSpec(
        num_scalar_prefetch=2,
        grid=(n_rows // rows,),
        in_specs=[
            pl.BlockSpec((rows, d), xin),
            pl.BlockSpec((1, d, de), wsel),
            pl.BlockSpec((1, d, de), wsel),
            pl.BlockSpec((1, de, d), wsel),
            pl.BlockSpec((1, 1, de), wsel),
            pl.BlockSpec((1, 1, de), wsel),
            pl.BlockSpec((1, 1, d), wsel),
        ],
        out_specs=pl.BlockSpec((rows, d), xrow),
    )
    return pl.pallas_call(
        _expert_kernel,
        grid_spec=grid_spec,
        out_shape=jax.ShapeDtypeStruct((n_rows, d), F32),
        compiler_params=_cparams(("arbitrary",)),
        name="moe_experts",
    )(blk_e, n_used, buf, wg, wl, wd, bg, bl, bd)


def _combine_kernel(dest_ref, gw_ref, h_ref, obuf, y_ref, g_sc, sem, *, topk):
    tm = h_ref.shape[0]

    def issue(t, c):
        for k in range(topk):
            _row_copy(obuf, dest_ref[k, t], g_sc.at[k], t, sem).start()
        return c

    lax.fori_loop(0, tm, issue, 0, unroll=DMA_UNROLL)

    def drain(t, c):
        for k in range(topk):
            _row_copy(obuf, 0, g_sc.at[k], 0, sem).wait()
        return c

    lax.fori_loop(0, tm, drain, 0, unroll=DMA_UNROLL)
    gw = gw_ref[...]
    ff = g_sc[0] * gw[:, 0:1]
    for k in range(1, topk):
        ff = ff + g_sc[k] * gw[:, k:k + 1]
    y_ref[...] = h_ref[...] + ff


def _combine(dest, gw_t, h, obuf, *, tm, topk):
    t, d = h.shape
    return pl.pallas_call(
        functools.partial(_combine_kernel, topk=topk),
        grid=(t // tm,),
        in_specs=[
            pl.BlockSpec((topk, tm), lambda i: (0, i), memory_space=pltpu.SMEM),
            pl.BlockSpec((tm, topk), lambda i: (i, 0)),
            pl.BlockSpec((tm, d), lambda i: (i, 0)),
            pl.BlockSpec(memory_space=pl.ANY),
        ],
        out_specs=pl.BlockSpec((tm, d), lambda i: (i, 0)),
        out_shape=jax.ShapeDtypeStruct((t, d), F32),
        scratch_shapes=[pltpu.VMEM((topk, tm, d), F32), pltpu.SemaphoreType.DMA(())],
        compiler_params=_cparams(("arbitrary",)),
        name="moe_combine",
    )(dest, gw_t, h, obuf)


def _split_gu_kernel(w_ref, wg_ref, wl_ref, *, seg):
    x = w_ref[0].astype(BF16)
    src = lax.broadcasted_iota(I32, (2 * seg, 2 * seg), 0)
    dst = lax.broadcasted_iota(I32, (2 * seg, 2 * seg), 1)
    wanted = jnp.where(dst < seg, 2 * dst, 2 * (dst - seg) + 1)
    pick = (src == wanted).astype(BF16)
    for n in range(x.shape[1] // (2 * seg)):
        both = jnp.dot(x[:, n * 2 * seg:(n + 1) * 2 * seg], pick, preferred_element_type=F32)
        wg_ref[0, :, n * seg:(n + 1) * seg] = both[:, :seg].astype(BF16)
        wl_ref[0, :, n * seg:(n + 1) * seg] = both[:, seg:].astype(BF16)


def _split_gu(w_gu):
    n_exp, d, de2 = w_gu.shape
    rt = min(d, 512)
    seg = 256
    assert d % rt == 0 and de2 % (2 * seg) == 0
    return pl.pallas_call(
        functools.partial(_split_gu_kernel, seg=seg),
        grid=(n_exp, d // rt),
        in_specs=[pl.BlockSpec((1, rt, de2), lambda e, r: (e, r, 0))],
        out_specs=[pl.BlockSpec((1, rt, de2 // 2), lambda e, r: (e, r, 0))] * 2,
        out_shape=[jax.ShapeDtypeStruct((n_exp, d, de2 // 2), BF16)] * 2,
        compiler_params=_cparams(("arbitrary", "arbitrary")),
        name="split_gu",
    )(w_gu)


def _dest_kernel(ps_ref, te_ref, rk_ref, dest_ref, *, n_exp):
    te = te_ref[...]
    dest = rk_ref[...]
    for e in range(n_exp):
        dest = dest + jnp.where(te == e, ps_ref[e], 0)
    dest_ref[...] = dest


def _dest_rows(pad_start, te, rk):
    return pl.pallas_call(
        functools.partial(_dest_kernel, n_exp=pad_start.shape[0]),
        in_specs=[pl.BlockSpec(memory_space=pltpu.SMEM),
                  pl.BlockSpec(memory_space=pltpu.VMEM),
                  pl.BlockSpec(memory_space=pltpu.VMEM)],
        out_specs=pl.BlockSpec(memory_space=pltpu.VMEM),
        out_shape=jax.ShapeDtypeStruct(te.shape, I32),
        name="moe_dest",
    )(pad_start, te, rk)


def _moe(h, hn, te, gw, rk, cnt, wts, *, rows, tm_io, topk):
    t, d = h.shape
    wg, wl, wd, bg, bl, bd = wts
    n_exp = wg.shape[0]
    counts = cnt[:, 0].astype(I32)
    padded = (counts + rows - 1) // rows * rows
    pad_end = jnp.cumsum(padded)
    pad_start = pad_end - padded
    dest = _dest_rows(pad_start, te, rk)
    n_blk = -(-(t * topk) // rows) + n_exp
    blk_lo = jnp.arange(n_blk, dtype=I32) * rows
    blk_e = jnp.minimum(
        jnp.sum((pad_end[None, :] <= blk_lo[:, None]).astype(I32), axis=1), n_exp - 1)
    n_used = (pad_end[-1:] // rows).astype(I32)
    fill = jnp.stack([pad_end, padded - counts]).astype(I32)
    buf = _dispatch(fill, n_used, dest, hn, rows=rows, n_blk=n_blk, tm=tm_io, topk=topk)
    obuf = _experts(blk_e, n_used, buf, wg, wl, wd, bg, bl, bd, rows=rows)
    return _combine(dest, gw.T, h, obuf, tm=tm_io, topk=topk)


def kernel(x_prompt, x_sample, cache_k, cache_v, state_h, state_conv, page_table, norm1_g, w_in,
           q_norm_g, k_norm_g, conv_w, conv_b, w_rg, b_rg, w_ig, b_ig, lru_lambda, w_attn_o,
           w_lru_o, w_out, norm2_g, w_router, b_router, w_gu, b_gu, w_dn, b_dn):
    depth = w_in.shape[0]
    assert depth == 1, "one layer per step is supported"
    bp, sp, d = x_prompt.shape
    nbd, sd, _ = x_sample.shape
    assert sd == 1, "decode handles one new token per sequence"
    page, n_heads, head_dim = cache_k.shape[2], cache_k.shape[3], cache_k.shape[4]
    aw = n_heads * head_dim
    lw = w_lru_o.shape[1]
    n_exp = w_router.shape[2]
    n_pages = page_table.shape[1]
    past_len = n_pages * page
    blk = MOBA_BLOCK
    assert sp % blk == 0 and past_len % blk == 0 and blk % page == 0
    assert LANES % head_dim == 0 and aw % LANES == 0 and page == LANES
    ppb = blk // page
    assert past_len // blk >= MOBA_TOPK

    w_in_bf = w_in[0].astype(BF16)
    wrg_bf, wig_bf = w_rg[0].astype(BF16), w_ig[0].astype(BF16)
    wao_bf, wlo_bf, wout_bf = w_attn_o[0].astype(BF16), w_lru_o[0].astype(BF16), w_out[0].astype(BF16)
    wg_bf, wl_bf = _split_gu(w_gu[0])
    wd_bf = w_dn[0].astype(BF16)
    bg = b_gu[0][:, None, 0::2]
    bl = b_gu[0][:, None, 1::2]
    bdn = b_dn[0][:, None, :]
    moe_w = (wg_bf, wl_bf, wd_bf, bg, bl, bdn)
    qg = jnp.tile(q_norm_g[0], n_heads).reshape(1, aw)
    kg = jnp.tile(k_norm_g[0], n_heads).reshape(1, aw)
    head_of = jnp.arange(aw, dtype=I32) // head_dim
    same_head = head_of[:, None] == head_of[None, :]
    bd_mean = (same_head.astype(F32) / head_dim).astype(BF16)
    g1 = norm1_g[0].reshape(1, d)
    g2 = norm2_g[0].reshape(1, d)
    wrt = w_router[0].T.astype(BF16)
    br = b_router[0].reshape(n_exp, 1)
    cw, cb = conv_w[0], conv_b[0].reshape(1, lw)
    brg, big, lam = b_rg[0].reshape(1, lw), b_ig[0].reshape(1, lw), lru_lambda[0].reshape(1, lw)
    n_tap = cw.shape[0]

    tp = bp * sp
    cos_p, sin_p = _rope_table(sp, 0, head_dim)
    xp2 = x_prompt.reshape(tp, d)
    (q_p, k_p, v_p, kbf_p, vt_p, km_p, xb_p, gy_p, sga_p, sgl_p) = _inproj(
        xp2, g1, w_in_bf, qg, kg, cos_p, sin_p, bd_mean,
        tm=blk, n_pos_tiles=sp // blk, aw=aw, lw=lw, head_dim=head_dim, attn_aux=True)
    nb = sp // blk
    attn_p = _attention(
        q_p.reshape(bp, sp, aw), kbf_p.reshape(bp, nb, blk, aw), vt_p.reshape(bp, nb, aw, blk),
        km_p.reshape(bp, nb, aw), head_dim=head_dim, topk=MOBA_TOPK).reshape(tp, aw)
    lru_p, hlast_p = _lru_prompt(
        xb_p, gy_p, jnp.zeros((bp, n_tap - 1, lw), F32), jnp.zeros((bp, 1, lw), F32),
        cw, cb, wrg_bf, brg, wig_bf, big, lam, b=bp, s=sp, ts=blk, pos0=0)
    h_p, hn_p, te_p, gw_p, rk_p, cnt_p = _merge(
        attn_p, lru_p, sga_p, sgl_p, xp2, wao_bf, wlo_bf, wout_bf, g2, wrt, br,
        tm=blk, n_exp=n_exp, topk=TOP_K)
    y_p = _moe(h_p, hn_p, te_p, gw_p, rk_p, cnt_p, moe_w, rows=512, tm_io=blk, topk=TOP_K)
    assert sp >= n_tap - 1
    conv_p = xb_p.reshape(bp, sp, lw)[:, sp - (n_tap - 1):]

    cos_d, sin_d = _rope_table(sd, past_len, head_dim)
    cos_d = jnp.tile(cos_d, (nbd, 1))
    sin_d = jnp.tile(sin_d, (nbd, 1))
    xd2 = x_sample.reshape(nbd, d)
    (q_d, k_d, v_d, _, _, _, xb_d, gy_d, sga_d, sgl_d) = _inproj(
        xd2, g1, w_in_bf, qg, kg, cos_d, sin_d, bd_mean,
        tm=nbd, n_pos_tiles=1, aw=aw, lw=lw, head_dim=head_dim, attn_aux=False)
    ckt = cache_k[0].transpose(0, 2, 3, 1)
    cvt = cache_v[0].transpose(0, 2, 3, 1)
    q_d3 = q_d.reshape(nbd, 1, aw)
    sel = _dec_select(page_table, q_d3, ckt, ppb=ppb, blk=blk, topk=MOBA_TOPK)[:, :, :MOBA_TOPK]
    attn_d = _dec_attention(
        page_table, sel, q_d3, k_d.reshape(nbd, 1, aw), v_d.reshape(nbd, 1, aw), ckt, cvt,
        ppb=ppb, topk=MOBA_TOPK).reshape(nbd, aw)
    cs_d = state_conv[0]
    lru_d, h_d = _lru_step(xb_d, gy_d, cs_d.transpose(1, 0, 2), state_h[0], cw, cb,
                           wrg_bf, brg, wig_bf, big, lam)
    hd_, hn_d, te_d, gw_d, rk_d, cnt_d = _merge(
        attn_d, lru_d, sga_d, sgl_d, xd2, wao_bf, wlo_bf, wout_bf, g2, wrt, br,
        tm=nbd, n_exp=n_exp, topk=TOP_K)
    y_d = _moe(hd_, hn_d, te_d, gw_d, rk_d, cnt_d, moe_w, rows=2 * SUBLANES, tm_io=nbd, topk=TOP_K)
    conv_d = jnp.concatenate([cs_d, xb_d.reshape(nbd, sd, lw)], axis=1)[:, sd:]

    return (
        y_p.reshape(bp, sp, d),
        y_d.reshape(nbd, sd, d),
        k_p.reshape(1, bp, sp, n_heads, head_dim),
        v_p.reshape(1, bp, sp, n_heads, head_dim),
        hlast_p.reshape(1, bp, lw),
        conv_p.reshape(1, bp, n_tap - 1, lw),
        k_d.reshape(1, nbd, sd, n_heads, head_dim),
        v_d.reshape(1, nbd, sd, n_heads, head_dim),
        h_d.reshape(1, nbd, lw),
        conv_d.reshape(1, nbd, n_tap - 1, lw),
    )
```

```python
import functools

import jax
import jax.numpy as jnp
from jax import lax
from jax.experimental import pallas as pl
from jax.experimental.pallas import tpu as pltpu

F32 = jnp.float32
BF16 = jnp.bfloat16
I32 = jnp.int32

MOBA_BLOCK = 256
MOBA_TOPK = 3
ROPE_THETA = 10000.0
LRU_C = 8.0
TOP_K = 4
SWIGLU_LIMIT = 7.0
SWIGLU_ALPHA = 1.702
NORM_EPS = 1e-6

LANES = 128
SUBLANES = 8
VMEM_LIMIT = 56 * 1024 * 1024
DMA_UNROLL = 8
NEG = float(jnp.finfo(jnp.float32).min)
LOG2E = 1.4426950408889634
HIGHEST = lax.Precision.HIGHEST
NT = (((1,), (1,)), ((), ()))


def _cparams(sem):
    return pltpu.CompilerParams(dimension_semantics=sem, vmem_limit_bytes=VMEM_LIMIT)


def _rope_table_kernel(inv_ref, cos_ref, sin_ref, *, pos0, rows, head_dim):
    i = pl.program_id(0)
    pos = lax.broadcasted_iota(I32, (rows, LANES), 0) + (pos0 + i * rows)
    lane = lax.broadcasted_iota(I32, (rows, LANES), 1)
    ang = pos.astype(F32) * inv_ref[...]
    first_half = (lane % head_dim) < (head_dim // 2)
    cos_ref[...] = jnp.cos(ang)
    s = jnp.sin(ang)
    sin_ref[...] = jnp.where(first_half, -s, s)


def _rope_table(n_pos, pos0, head_dim):
    half = head_dim // 2
    inv = ROPE_THETA ** (-jnp.arange(half, dtype=F32) * 2.0 / head_dim)
    inv_l = jnp.tile(inv, LANES // half).reshape(1, LANES)
    n_pad = -(-n_pos // SUBLANES) * SUBLANES
    rows = min(n_pad, 512)
    assert n_pad % rows == 0
    cos, sin = pl.pallas_call(
        functools.partial(_rope_table_kernel, pos0=pos0, rows=rows, head_dim=head_dim),
        grid=(n_pad // rows,),
        in_specs=[pl.BlockSpec((1, LANES), lambda i: (0, 0))],
        out_specs=[pl.BlockSpec((rows, LANES), lambda i: (i, 0))] * 2,
        out_shape=[jax.ShapeDtypeStruct((n_pad, LANES), F32)] * 2,
        name="rope_table",
    )(inv_l)
    return cos[:n_pos], sin[:n_pos]


def _inproj_kernel(x_ref, g1_ref, w_ref, qg_ref, kg_ref, cos_ref, sin_ref, bd_ref,
                   q_ref, k_ref, v_ref, kbf_ref, vt_ref, km_ref, xb_ref, gy_ref, sga_ref, sgl_ref,
                   *, aw, lw, d, head_dim, attn_aux):
    x = x_ref[...]
    xn = x * lax.rsqrt(jnp.mean(x * x, axis=-1, keepdims=True) + NORM_EPS) * g1_ref[...]
    xn = xn.astype(BF16)
    tm = x.shape[0]

    def proj(lo, width):
        return jnp.dot(xn, w_ref[:, lo:lo + width], preferred_element_type=F32)

    reps = aw // LANES
    cos = jnp.concatenate([cos_ref[...]] * reps, axis=1)
    sin = jnp.concatenate([sin_ref[...]] * reps, axis=1)
    lane = lax.broadcasted_iota(I32, (tm, aw), 1)
    first_half = (lane % head_dim) < (head_dim // 2)
    bd = bd_ref[...]

    def head_norm_rope(t, g):
        ss = t * t
        hi = ss.astype(BF16)
        lo = (ss - hi.astype(F32)).astype(BF16)
        ms = (jnp.dot(hi, bd, preferred_element_type=F32)
              + jnp.dot(lo, bd, preferred_element_type=F32))
        tn = t * lax.rsqrt(ms + NORM_EPS) * g
        up = pltpu.roll(tn, aw - head_dim // 2, 1)
        dn = pltpu.roll(tn, head_dim // 2, 1)
        return tn * cos + jnp.where(first_half, up, dn) * sin

    q_ref[...] = head_norm_rope(proj(0, aw), qg_ref[...])
    k = head_norm_rope(proj(aw, aw), kg_ref[...])
    k_ref[...] = k
    v = proj(2 * aw, aw)
    v_ref[...] = v
    if attn_aux:
        kbf_ref[...] = k.astype(BF16)
        km_ref[0] = jnp.mean(k, axis=0, keepdims=True)
        vt_ref[0] = v.T.astype(BF16)
    else:
        kbf_ref[...] = jnp.zeros(kbf_ref.shape, BF16)
        km_ref[...] = jnp.zeros(km_ref.shape, F32)
        vt_ref[...] = jnp.zeros(vt_ref.shape, BF16)
    xb_ref[...] = proj(3 * aw, lw)
    gy_ref[...] = jax.nn.gelu(proj(3 * aw + lw, lw))
    sga_ref[...] = jax.nn.sigmoid(proj(3 * aw + 2 * lw, d))
    sgl_ref[...] = jax.nn.sigmoid(proj(3 * aw + 2 * lw + d, d))


def _inproj(x2d, g1, w_bf, qg, kg, cos, sin, bd, *, tm, n_pos_tiles, aw, lw, head_dim, attn_aux):
    t, d = x2d.shape
    nt = t // tm
    row = lambda i: (i, 0)
    const = lambda i: (0, 0)
    pos = lambda i: (i % n_pos_tiles, 0)
    outs = pl.pallas_call(
        functools.partial(_inproj_kernel, aw=aw, lw=lw, d=d, head_dim=head_dim, attn_aux=attn_aux),
        grid=(nt,),
        in_specs=[
            pl.BlockSpec((tm, d), row),
            pl.BlockSpec((1, d), const),
            pl.BlockSpec(w_bf.shape, const),
            pl.BlockSpec((1, aw), const),
            pl.BlockSpec((1, aw), const),
            pl.BlockSpec((tm, LANES), pos),
            pl.BlockSpec((tm, LANES), pos),
            pl.BlockSpec((aw, aw), const),
        ],
        out_specs=[
            pl.BlockSpec((tm, aw), row),
            pl.BlockSpec((tm, aw), row),
            pl.BlockSpec((tm, aw), row),
            pl.BlockSpec((tm, aw), row),
            pl.BlockSpec((1, aw, tm), lambda i: (i, 0, 0)),
            pl.BlockSpec((1, 1, aw), lambda i: (i, 0, 0)),
            pl.BlockSpec((tm, lw), row),
            pl.BlockSpec((tm, lw), row),
            pl.BlockSpec((tm, d), row),
            pl.BlockSpec((tm, d), row),
        ],
        out_shape=[
            jax.ShapeDtypeStruct((t, aw), F32),
            jax.ShapeDtypeStruct((t, aw), F32),
            jax.ShapeDtypeStruct((t, aw), F32),
            jax.ShapeDtypeStruct((t, aw), BF16),
            jax.ShapeDtypeStruct((nt, aw, tm), BF16),
            jax.ShapeDtypeStruct((nt, 1, aw), F32),
            jax.ShapeDtypeStruct((t, lw), F32),
            jax.ShapeDtypeStruct((t, lw), F32),
            jax.ShapeDtypeStruct((t, d), F32),
            jax.ShapeDtypeStruct((t, d), F32),
        ],
        compiler_params=_cparams(("arbitrary",)),
        name="inproj",
    )(x2d, g1, w_bf, qg, kg, cos, sin, bd)
    return outs


def _top_rows(g, n_rows, topk, valid_floor):
    row = lax.broadcasted_iota(I32, g.shape, 0)
    sel = jnp.zeros(g.shape, dtype=jnp.bool_)
    idxs = []
    vals = []
    for _ in range(topk):
        mx = jnp.max(g, axis=0, keepdims=True)
        idx = jnp.min(jnp.where(g == mx, row, n_rows), axis=0, keepdims=True)
        hit = row == idx
        sel = sel | (hit & (mx > valid_floor))
        g = jnp.where(hit, -jnp.inf, g)
        idxs.append(idx)
        vals.append(mx)
    return idxs, vals, sel


def _attn_kernel(q_ref, k_ref, vt_ref, km_ref, o_ref, sel_ref, sa_ref, sb_ref,
                 *, nb, blk, head_dim, topk, n_pairs):
    i = pl.program_id(2)
    n_h = LANES // head_dim
    n_heads = n_pairs * n_h
    lane = lax.broadcasted_iota(I32, (blk, LANES), 1)
    brow = lax.broadcasted_iota(I32, (nb, blk), 0)
    qscale = head_dim ** -0.5 * LOG2E
    qs = []
    for pp in range(n_pairs):
        lanes = slice(pp * LANES, (pp + 1) * LANES)
        q = q_ref[0, :, lanes]
        km = km_ref[0, :, lanes].astype(BF16)
        parts = []
        for hh in range(n_h):
            in_head = (lane >= hh * head_dim) & (lane < (hh + 1) * head_dim)
            qm = jnp.where(in_head, q, 0.0)
            gate = lax.dot_general(km, qm.astype(BF16), NT, preferred_element_type=F32)
            gate = jnp.where(brow < i, gate, NEG)
            _, _, sel = _top_rows(gate, nb, topk, NEG)
            sel_ref[pp * n_h + hh] = sel.astype(F32)
            parts.append((qm * qscale).astype(BF16))
        qs.append(jnp.concatenate(parts, axis=0))

    def scores(j, pp):
        return lax.dot_general(k_ref[0, j, :, pp * LANES:(pp + 1) * LANES], qs[pp], NT,
                               preferred_element_type=F32)

    def update(j, pp, hh, s, m, l, acc):
        m_new = jnp.maximum(m, jnp.max(s, axis=0, keepdims=True))
        alpha = jnp.exp2(m - m_new)
        p = jnp.exp2(s - m_new)
        l = alpha * l + jnp.sum(p, axis=0, keepdims=True)
        lo = pp * LANES + hh * head_dim
        pv = jnp.dot(vt_ref[0, j, lo:lo + head_dim, :], p.astype(BF16), preferred_element_type=F32)
        return m_new, l, alpha * acc + pv

    causal = (lax.broadcasted_iota(I32, (blk, blk), 0) <= lax.broadcasted_iota(I32, (blk, blk), 1))
    state = []
    for pp in range(n_pairs):
        s_own = scores(i, pp)
        for hh in range(n_h):
            s = jnp.where(causal, s_own[:, hh * blk:(hh + 1) * blk], NEG)
            state.extend(update(i, pp, hh, s, jnp.full((1, blk), NEG, F32),
                                jnp.zeros((1, blk), F32), jnp.zeros((head_dim, blk), F32)))

    def soft_step(j, valid, s_ref, st):
        out = []
        for pp in range(n_pairs):
            for hh in range(n_h):
                n = pp * n_h + hh
                m, l, acc = st[3 * n:3 * n + 3]
                keep = sel_ref[n, pl.ds(j, 1), :] > 0.5
                if valid is not None:
                    keep = jnp.logical_and(keep, valid)
                s = jnp.where(keep, s_ref[pp, :, hh * blk:(hh + 1) * blk], NEG)
                out.extend(update(j, pp, hh, s, m, l, acc))
        return out

    for pp in range(n_pairs):
        sa_ref[pp] = scores(0, pp)

    def body(jj, st):
        j0 = 2 * jj
        j1 = j0 + 1
        for pp in range(n_pairs):
            sb_ref[pp] = scores(j1, pp)
        st = soft_step(j0, None, sa_ref, st)
        j2 = jnp.minimum(j1 + 1, nb - 1)
        for pp in range(n_pairs):
            sa_ref[pp] = scores(j2, pp)
        return tuple(soft_step(j1, j1 < i, sb_ref, st))

    final = lax.fori_loop(0, (i + 1) // 2, body, tuple(state))
    outs = [final[3 * n + 2] / final[3 * n + 1] for n in range(n_heads)]
    o_ref[0] = jnp.concatenate(outs, axis=0).T.astype(BF16)


def _attention(q3, k4, vt4, km3, *, head_dim, topk):
    b, s, aw = q3.shape
    nb, blk = k4.shape[1], k4.shape[2]
    n_pairs = 2 if aw % (2 * LANES) == 0 else 1
    gl = n_pairs * LANES
    return pl.pallas_call(
        functools.partial(_attn_kernel, nb=nb, blk=blk, head_dim=head_dim, topk=topk,
                          n_pairs=n_pairs),
        grid=(b, aw // gl, nb),
        in_specs=[
            pl.BlockSpec((1, blk, gl), lambda bi, hp, i: (bi, i, hp)),
            pl.BlockSpec((1, nb, blk, gl), lambda bi, hp, i: (bi, 0, 0, hp)),
            pl.BlockSpec((1, nb, gl, blk), lambda bi, hp, i: (bi, 0, hp, 0)),
            pl.BlockSpec((1, nb, gl), lambda bi, hp, i: (bi, 0, hp)),
        ],
        out_specs=pl.BlockSpec((1, blk, gl), lambda bi, hp, i: (bi, i, hp)),
        out_shape=jax.ShapeDtypeStruct((b, s, aw), BF16),
        scratch_shapes=[pltpu.VMEM((gl // head_dim, nb, blk), F32),
                        pltpu.VMEM((n_pairs, blk, (LANES // head_dim) * blk), F32),
                        pltpu.VMEM((n_pairs, blk, (LANES // head_dim) * blk), F32)],
        compiler_params=_cparams(("arbitrary", "arbitrary", "arbitrary")),
        name="moba_attn",
    )(q3, k4, vt4, km3)


def _col_bcast(row):
    return jnp.broadcast_to(row, (LANES, row.shape[1])).T


def _top_lanes(g, n_cols, topk):
    col = lax.broadcasted_iota(I32, g.shape, 1)
    idxs = []
    for _ in range(topk):
        mx = jnp.max(g, axis=1, keepdims=True)
        idx = jnp.min(jnp.where(g == mx, col, n_cols), axis=1, keepdims=True)
        g = jnp.where(col == idx, -jnp.inf, g)
        idxs.append(idx)
    return idxs


def _dec_gate_kernel(pt_ref, q_ref, *refs, n_pp, ppb, n_groups, blk, topk, head_dim):
    page_refs = refs[:n_pp]
    sel_ref, qcol_sc, g_sc = refs[n_pp:]
    g = pl.program_id(1)
    n_heads = qcol_sc.shape[0]
    gw = g_sc.shape[1]
    per_step = n_pp // ppb

    @pl.when(g == 0)
    def _():
        qc = _col_bcast(q_ref[0])
        for h in range(n_heads):
            qcol_sc[h] = qc[h * head_dim:(h + 1) * head_dim]
        g_sc[...] = jnp.full(g_sc.shape, NEG, F32)

    lane = lax.broadcasted_iota(I32, (n_heads, gw), 1)
    gates = g_sc[...]
    qcol = qcol_sc[...]
    for u in range(per_step):
        tot = (page_refs[u * ppb][0] * qcol).sum(axis=1)
        for w in range(1, ppb):
            tot = tot + (page_refs[u * ppb + w][0] * qcol).sum(axis=1)
        score = jnp.sum(tot, axis=1, keepdims=True) * (1.0 / blk)
        gates = jnp.where(lane == g * per_step + u, score, gates)
    g_sc[...] = gates

    @pl.when(g == n_groups - 1)
    def _():
        idxs = _top_lanes(g_sc[...], gw, topk)
        olane = lax.broadcasted_iota(I32, (n_heads, LANES), 1)
        out = jnp.zeros((n_heads, LANES), I32)
        for r, idx in enumerate(idxs):
            out = jnp.where(olane == r, idx, out)
        sel_ref[0] = out


def _dec_select(page_table, q3, cache_kt, *, ppb, blk, topk):
    nbd, n_pages = page_table.shape
    _, n_heads, head_dim, page = cache_kt.shape
    aw = n_heads * head_dim
    n_pp = ppb
    while n_pp * 2 <= 16 and n_pages % (n_pp * 2) == 0:
        n_pp *= 2
    n_groups = n_pages // n_pp
    nbp = n_pages // ppb
    gw = -(-nbp // LANES) * LANES

    def page_spec(u):
        return pl.BlockSpec((1, n_heads, head_dim, page),
                            lambda bi, g, pt: (pt[bi, g * n_pp + u], 0, 0, 0))

    grid_spec = pltpu.PrefetchScalarGridSpec(
        num_scalar_prefetch=1,
        grid=(nbd, n_groups),
        in_specs=[pl.BlockSpec((1, 1, aw), lambda bi, g, pt: (bi, 0, 0))]
        + [page_spec(u) for u in range(n_pp)],
        out_specs=pl.BlockSpec((1, n_heads, LANES), lambda bi, g, pt: (bi, 0, 0)),
        scratch_shapes=[pltpu.VMEM((n_heads, head_dim, LANES), F32),
                        pltpu.VMEM((n_heads, gw), F32)],
    )
    return pl.pallas_call(
        functools.partial(_dec_gate_kernel, n_pp=n_pp, ppb=ppb, n_groups=n_groups, blk=blk,
                          topk=topk, head_dim=head_dim),
        grid_spec=grid_spec,
        out_shape=jax.ShapeDtypeStruct((nbd, n_heads, LANES), I32),
        compiler_params=_cparams(("arbitrary", "arbitrary")),
        name="dec_block_select",
    )(page_table, q3, *([cache_kt] * n_pp))


def _dec_attn_kernel(pt_ref, sel_ref, q_ref, kn_ref, vn_ref, *refs, n_heads, head_dim, ppb, topk):
    n_pg = n_heads * ppb
    k_refs = refs[:n_pg]
    v_refs = refs[n_pg:2 * n_pg]
    o_ref, qcol_sc, m_sc, l_sc, acc_sc = refs[2 * n_pg:]
    r = pl.program_id(1)
    scale = head_dim ** -0.5
    lane = lax.broadcasted_iota(I32, (head_dim, LANES), 1)

    @pl.when(r == 0)
    def _():
        qc = _col_bcast(q_ref[0])
        kc = _col_bcast(kn_ref[0])
        vc = _col_bcast(vn_ref[0])
        for h in range(n_heads):
            rows = slice(h * head_dim, (h + 1) * head_dim)
            qcol_sc[h] = qc[rows]
            m_sc[h:h + 1, :] = jnp.sum(qc[rows] * kc[rows], axis=0, keepdims=True) * scale
            l_sc[h:h + 1, :] = jnp.ones((1, LANES), F32)
            acc_sc[h] = jnp.where(lane == 0, vc[rows], 0.0)

    for h in range(n_heads):
        qc = qcol_sc[h]
        m = m_sc[h:h + 1, :]
        ss = [jnp.sum(k_refs[h * ppb + w][0, 0] * qc, axis=0, keepdims=True) * scale
              for w in range(ppb)]
        blk_max = ss[0]
        for s in ss[1:]:
            blk_max = jnp.maximum(blk_max, s)
        m_new = jnp.maximum(m, jnp.max(blk_max, axis=1, keepdims=True))
        alpha = jnp.exp(m - m_new)
        l = alpha * l_sc[h:h + 1, :]
        acc = alpha * acc_sc[h]
        for w in range(ppb):
            p = jnp.exp(ss[w] - m_new)
            l = l + jnp.sum(p, axis=1, keepdims=True)
            acc = acc + v_refs[h * ppb + w][0, 0] * p
        m_sc[h:h + 1, :] = m_new
        l_sc[h:h + 1, :] = l
        acc_sc[h] = acc

    @pl.when(r == topk - 1)
    def _():
        cols = [jnp.broadcast_to(
            jnp.sum(acc_sc[h], axis=1, keepdims=True) / l_sc[h:h + 1, 0:1], (head_dim, LANES))
            for h in range(n_heads)]
        o_ref[0] = jnp.concatenate(cols, axis=0).T[0:1]


def _dec_attention(page_table, sel, q3, kn3, vn3, cache_kt, cache_vt, *, ppb, topk):
    nbd = page_table.shape[0]
    _, n_heads, head_dim, page = cache_kt.shape
    aw = n_heads * head_dim

    def page_spec(h, w):
        return pl.BlockSpec(
            (1, 1, head_dim, page),
            lambda bi, r, pt, sl: (pt[bi, ppb * sl[bi, h, r] + w], h, 0, 0))

    vec = pl.BlockSpec((1, 1, aw), lambda bi, r, pt, sl: (bi, 0, 0))
    pages = [page_spec(h, w) for h in range(n_heads) for w in range(ppb)]
    grid_spec = pltpu.PrefetchScalarGridSpec(
        num_scalar_prefetch=2,
        grid=(nbd, topk),
        in_specs=[vec, vec, vec] + pages + pages,
        out_specs=vec,
        scratch_shapes=[pltpu.VMEM((n_heads, head_dim, LANES), F32),
                        pltpu.VMEM((n_heads, LANES), F32),
                        pltpu.VMEM((n_heads, LANES), F32),
                        pltpu.VMEM((n_heads, head_dim, LANES), F32)],
    )
    n_pg = n_heads * ppb
    return pl.pallas_call(
        functools.partial(_dec_attn_kernel, n_heads=n_heads, head_dim=head_dim, ppb=ppb, topk=topk),
        grid_spec=grid_spec,
        out_shape=jax.ShapeDtypeStruct((nbd, 1, aw), F32),
        compiler_params=_cparams(("arbitrary", "arbitrary")),
        name="dec_attn",
    )(page_table, sel, q3, kn3, vn3, *([cache_kt] * n_pg), *([cache_vt] * n_pg))


def _lru_gates(xc, wrg_ref, brg, wig_ref, big, lam, first_pos_mask):
    nblk, bw = wrg_ref.shape[0], wrg_ref.shape[1]
    xcb = xc.astype(BF16)
    rz = jnp.concatenate(
        [jnp.dot(xcb[:, n * bw:(n + 1) * bw], wrg_ref[n], preferred_element_type=F32)
         for n in range(nblk)], axis=1)
    iz = jnp.concatenate(
        [jnp.dot(xcb[:, n * bw:(n + 1) * bw], wig_ref[n], preferred_element_type=F32)
         for n in range(nblk)], axis=1)
    r = jax.nn.sigmoid(rz + brg)
    ig = jax.nn.sigmoid(iz + big)
    neg_lam = -lam
    softplus = jnp.maximum(neg_lam, 0.0) + jnp.log1p(jnp.exp(-jnp.abs(neg_lam)))
    log_a = -LRU_C * r * softplus
    a = jnp.exp(log_a)
    mult = jnp.sqrt(jnp.tanh(-log_a) * (a * a + 1.0))
    if first_pos_mask is not None:
        mult = jnp.where(first_pos_mask, 1.0, mult)
    return a, xc * ig * mult


def _lru_kernel(xb_ref, gy_ref, cs_ref, h0_ref, cw_ref, cb_ref, wrg_ref, brg_ref, wig_ref, big_ref,
                lam_ref, out_ref, hlast_ref, xext, a_sc, u_sc, hcar, *, ts, pos0):
    t = pl.program_id(1)
    w = xb_ref.shape[1]
    halo = SUBLANES
    n_tap = cw_ref.shape[0]

    @pl.when(t == 0)
    def _():
        xext[0:halo, :] = jnp.zeros((halo, w), F32)
        xext[halo - (n_tap - 1):halo, :] = cs_ref[0]
        hcar[...] = h0_ref[0]

    xext[halo:halo + ts, :] = xb_ref[...]
    xc = cb_ref[...]
    for j in range(n_tap):
        lo = halo - (n_tap - 1) + j
        xc = xc + xext[lo:lo + ts, :] * cw_ref[j:j + 1, :]
    xext[0:halo, :] = xext[ts:ts + halo, :]

    row = lax.broadcasted_iota(I32, (ts, w), 0)
    first = (row + t * ts + pos0) == 0
    a, u = _lru_gates(xc, wrg_ref, brg_ref[...], wig_ref, big_ref[...], lam_ref[...], first)
    a_sc[...] = a
    u_sc[...] = u

    sub = lax.broadcasted_iota(I32, (SUBLANES, LANES), 0)
    carry = hcar[...]
    ends = []
    for lc in range(w // LANES):
        cols = slice(lc * LANES, (lc + 1) * LANES)
        state = jnp.broadcast_to(carry[:, cols], (SUBLANES, LANES))
        for g2 in range(ts // (2 * SUBLANES)):
            pair = []
            for g in (2 * g2, 2 * g2 + 1):
                rows = slice(g * SUBLANES, (g + 1) * SUBLANES)
                av = a_sc[rows, cols]
                uv = u_sc[rows, cols]
                for dist in (1, 2, 4):
                    a_up = jnp.where(sub >= dist, pltpu.roll(av, dist, 0), 1.0)
                    u_up = jnp.where(sub >= dist, pltpu.roll(uv, dist, 0), 0.0)
                    uv = uv + av * u_up
                    av = av * a_up
                h = uv + av * state
                state = jnp.broadcast_to(h[SUBLANES - 1:SUBLANES, :], (SUBLANES, LANES))
                pair.append(h)
            rows2 = slice(2 * g2 * SUBLANES, (2 * g2 + 2) * SUBLANES)
            out_ref[rows2, cols] = (jnp.concatenate(pair, axis=0) * gy_ref[rows2, cols]).astype(BF16)
        ends.append(state[0:1, :])
    carry = jnp.concatenate(ends, axis=1)
    hcar[...] = carry
    hlast_ref[0] = carry


def _lru_prompt(xb, gy, cs, h0, cw, cb, wrg, brg, wig, big, lam, *, b, s, ts, pos0):
    w = xb.shape[1]
    nt = s // ts
    assert ts % (2 * SUBLANES) == 0
    n_tap = cw.shape[0]
    row = lambda bi, t: (bi * nt + t, 0)
    const2 = lambda bi, t: (0, 0)
    const3 = lambda bi, t: (0, 0, 0)
    per_b = lambda bi, t: (bi, 0, 0)
    return pl.pallas_call(
        functools.partial(_lru_kernel, ts=ts, pos0=pos0),
        grid=(b, nt),
        in_specs=[
            pl.BlockSpec((ts, w), row),
            pl.BlockSpec((ts, w), row),
            pl.BlockSpec((1, n_tap - 1, w), per_b),
            pl.BlockSpec((1, 1, w), per_b),
            pl.BlockSpec(cw.shape, const2),
            pl.BlockSpec((1, w), const2),
            pl.BlockSpec(wrg.shape, const3),
            pl.BlockSpec((1, w), const2),
            pl.BlockSpec(wig.shape, const3),
            pl.BlockSpec((1, w), const2),
            pl.BlockSpec((1, w), const2),
        ],
        out_specs=[pl.BlockSpec((ts, w), row), pl.BlockSpec((1, 1, w), per_b)],
        out_shape=[jax.ShapeDtypeStruct((b * s, w), BF16), jax.ShapeDtypeStruct((b, 1, w), F32)],
        scratch_shapes=[
            pltpu.VMEM((ts + SUBLANES, w), F32),
            pltpu.VMEM((ts, w), F32),
            pltpu.VMEM((ts, w), F32),
            pltpu.VMEM((1, w), F32),
        ],
        compiler_params=_cparams(("arbitrary", "arbitrary")),
        name="rglru_scan",
    )(xb, gy, cs, h0, cw, cb, wrg, brg, wig, big, lam)


def _lru_step_kernel(xb_ref, gy_ref, cs_ref, h0_ref, cw_ref, cb_ref, wrg_ref, brg_ref, wig_ref,
                     big_ref, lam_ref, out_ref, h_ref):
    n_tap = cw_ref.shape[0]
    xc = cb_ref[...]
    for j in range(n_tap - 1):
        xc = xc + cs_ref[j] * cw_ref[j:j + 1, :]
    xc = xc + xb_ref[...] * cw_ref[n_tap - 1:n_tap, :]
    a, u = _lru_gates(xc, wrg_ref, brg_ref[...], wig_ref, big_ref[...], lam_ref[...], None)
    h = a * h0_ref[...] + u
    h_ref[...] = h
    out_ref[...] = h * gy_ref[...]


def _lru_step(xb, gy, cs_t, h0, cw, cb, wrg, brg, wig, big, lam):
    n, w = xb.shape
    return pl.pallas_call(
        _lru_step_kernel,
        out_shape=[jax.ShapeDtypeStruct((n, w), F32), jax.ShapeDtypeStruct((n, w), F32)],
        name="rglru_step",
    )(xb, gy, cs_t, h0, cw, cb, wrg, brg, wig, big, lam)


def _merge_kernel(attn_ref, lru_ref, sga_ref, sgl_ref, x_ref, wao_ref, wlo_ref, wout_ref, g2_ref,
                  wrt_ref, br_ref, h_ref, hn_ref, te_ref, gw_ref, rk_ref, cnt_ref, carry_sc,
                  *, n_exp, topk):
    i = pl.program_id(0)
    tm = x_ref.shape[0]

    @pl.when(i == 0)
    def _():
        carry_sc[...] = jnp.zeros(carry_sc.shape, F32)

    att = jnp.dot(attn_ref[...].astype(BF16), wao_ref[...], preferred_element_type=F32)
    lru = jnp.dot(lru_ref[...].astype(BF16), wlo_ref[...], preferred_element_type=F32)
    mixed = sga_ref[...] * att + sgl_ref[...] * lru
    h = x_ref[...] + jnp.dot(mixed.astype(BF16), wout_ref[...], preferred_element_type=F32)
    h_ref[...] = h
    hn = h * lax.rsqrt(jnp.mean(h * h, axis=-1, keepdims=True) + NORM_EPS) * g2_ref[...]
    hn_ref[...] = hn

    logits = lax.dot_general(wrt_ref[...], hn.astype(BF16), NT,
                             preferred_element_type=F32) + br_ref[...]
    idxs, vals, sel = _top_rows(logits, n_exp, topk, -jnp.inf)
    top_v = jnp.concatenate(vals, axis=0)
    ex = jnp.exp(top_v - top_v[0:1])
    gw_ref[...] = ex / jnp.sum(ex, axis=0, keepdims=True)
    te_ref[...] = jnp.concatenate(idxs, axis=0)

    before = lax.broadcasted_iota(I32, (tm, tm), 0) < lax.broadcasted_iota(I32, (tm, tm), 1)
    sel_bf = sel.astype(BF16)
    prior = jnp.dot(sel_bf, before.astype(BF16), preferred_element_type=F32)
    base = prior + carry_sc[:, 0:1]
    erow = lax.broadcasted_iota(I32, (n_exp, tm), 0)
    ranks = [jnp.sum(jnp.where(erow == idx, base, 0.0), axis=0, keepdims=True) for idx in idxs]
    rk_ref[...] = jnp.concatenate(ranks, axis=0).astype(I32)
    carry_sc[...] = carry_sc[...] + jnp.sum(sel.astype(F32), axis=1, keepdims=True)
    cnt_ref[...] = carry_sc[...]


def _merge(attn, lru, sga, sgl, x2d, wao, wlo, wout, g2, wrt, br, *, tm, n_exp, topk):
    t, d = x2d.shape
    aw, lw = attn.shape[1], lru.shape[1]
    row = lambda i: (i, 0)
    col = lambda i: (0, i)
    const = lambda i: (0, 0)
    return pl.pallas_call(
        functools.partial(_merge_kernel, n_exp=n_exp, topk=topk),
        grid=(t // tm,),
        in_specs=[
            pl.BlockSpec((tm, aw), row),
            pl.BlockSpec((tm, lw), row),
            pl.BlockSpec((tm, d), row),
            pl.BlockSpec((tm, d), row),
            pl.BlockSpec((tm, d), row),
            pl.BlockSpec((aw, d), const),
            pl.BlockSpec((lw, d), const),
            pl.BlockSpec((d, d), const),
            pl.BlockSpec((1, d), const),
            pl.BlockSpec((n_exp, d), const),
            pl.BlockSpec((n_exp, 1), const),
        ],
        out_specs=[
            pl.BlockSpec((tm, d), row),
            pl.BlockSpec((tm, d), row),
            pl.BlockSpec((topk, tm), col),
            pl.BlockSpec((topk, tm), col),
            pl.BlockSpec((topk, tm), col),
            pl.BlockSpec((n_exp, LANES), const),
        ],
        out_shape=[
            jax.ShapeDtypeStruct((t, d), F32),
            jax.ShapeDtypeStruct((t, d), F32),
            jax.ShapeDtypeStruct((topk, t), I32),
            jax.ShapeDtypeStruct((topk, t), F32),
            jax.ShapeDtypeStruct((topk, t), I32),
            jax.ShapeDtypeStruct((n_exp, LANES), F32),
        ],
        scratch_shapes=[pltpu.VMEM((n_exp, LANES), F32)],
        compiler_params=_cparams(("arbitrary",)),
        name="merge_router",
    )(attn, lru, sga, sgl, x2d, wao, wlo, wout, g2, wrt, br)


def _row_copy(src, src_row, dst, dst_row, sem):
    return pltpu.make_async_copy(src.at[pl.ds(src_row, 1)], dst.at[pl.ds(dst_row, 1)], sem)


def _fill_copies(fill_ref, nu_ref, hn_ref, buf_out, sem, *, rows, n_blk):
    tm = hn_ref.shape[0]
    n_exp = fill_ref.shape[1]
    piece = min(rows, tm)
    assert rows & (rows - 1) == 0 and rows // 2 <= tm and rows % piece == 0
    assert rows % SUBLANES == 0

    def run(method):
        def per_expert(e, c):
            end, n = fill_ref[0, e], fill_ref[1, e]
            for r in range(SUBLANES - 1):
                @pl.when(r < (n & (SUBLANES - 1)))
                def _():
                    getattr(_row_copy(hn_ref, 0, buf_out, end - n + r, sem), method)()

            for b in range(SUBLANES.bit_length() - 1, rows.bit_length() - 1):
                size = 1 << b

                @pl.when(((n >> b) & 1) == 1)
                def _():
                    at = pl.multiple_of(end - ((n >> (b + 1)) << (b + 1)) - size, SUBLANES)
                    getattr(pltpu.make_async_copy(
                        hn_ref.at[pl.ds(0, size)], buf_out.at[pl.ds(at, size)], sem), method)()
            return c

        lax.fori_loop(0, n_exp, per_expert, 0)

        def per_block(bi, c):
            for p in range(rows // piece):
                getattr(pltpu.make_async_copy(
                    hn_ref.at[pl.ds(0, piece)],
                    buf_out.at[pl.ds(pl.multiple_of(bi * rows + p * piece, SUBLANES), piece)],
                    sem), method)()
            return c

        lax.fori_loop(nu_ref[0], n_blk, per_block, 0)

    return functools.partial(run, "start"), functools.partial(run, "wait")


def _dispatch_kernel(fill_ref, nu_ref, dest_ref, hn_ref, buf_out, sem, fill_sem, *, topk, rows,
                     n_blk):
    tm = hn_ref.shape[0]
    start_fill, wait_fill = _fill_copies(fill_ref, nu_ref, hn_ref, buf_out, fill_sem,
                                         rows=rows, n_blk=n_blk)

    @pl.when(pl.program_id(0) == 0)
    def _():
        start_fill()

    def issue(t, c):
        for k in range(topk):
            _row_copy(hn_ref, t, buf_out, dest_ref[k, t], sem).start()
        return c

    lax.fori_loop(0, tm, issue, 0, unroll=DMA_UNROLL)

    def drain(t, c):
        for k in range(topk):
            _row_copy(hn_ref, 0, buf_out, 0, sem).wait()
        return c

    lax.fori_loop(0, tm, drain, 0, unroll=DMA_UNROLL)

    @pl.when(pl.program_id(0) == 0)
    def _():
        wait_fill()


def _dispatch(fill, n_used, dest, hn, *, rows, n_blk, tm, topk):
    t, d = hn.shape
    grid_spec = pltpu.PrefetchScalarGridSpec(
        num_scalar_prefetch=2,
        grid=(t // tm,),
        in_specs=[
            pl.BlockSpec((topk, tm), lambda i, fl, nu: (0, i), memory_space=pltpu.SMEM),
            pl.BlockSpec((tm, d), lambda i, fl, nu: (i, 0)),
        ],
        out_specs=pl.BlockSpec(memory_space=pl.ANY),
        scratch_shapes=[pltpu.SemaphoreType.DMA(()), pltpu.SemaphoreType.DMA(())],
    )
    return pl.pallas_call(
        functools.partial(_dispatch_kernel, topk=topk, rows=rows, n_blk=n_blk),
        grid_spec=grid_spec,
        out_shape=jax.ShapeDtypeStruct((n_blk * rows, d), hn.dtype),
        compiler_params=_cparams(("arbitrary",)),
        name="moe_dispatch",
    )(fill, n_used, dest, hn)


def _expert_kernel(be_ref, nu_ref, x_ref, wg_ref, wl_ref, wd_ref, bg_ref, bl_ref, bd_ref, o_ref):
    i = pl.program_id(0)

    @pl.when(i < nu_ref[0])
    def _():
        x = x_ref[...].astype(BF16)
        g = jnp.dot(x, wg_ref[0], preferred_element_type=F32) + bg_ref[0]
        lin = jnp.dot(x, wl_ref[0], preferred_element_type=F32) + bl_ref[0]
        glu = jnp.minimum(g, SWIGLU_LIMIT)
        lin = jnp.clip(lin, -SWIGLU_LIMIT, SWIGLU_LIMIT)
        act = glu * jax.nn.sigmoid(SWIGLU_ALPHA * glu) * (lin + 1.0)
        o_ref[...] = jnp.dot(act.astype(BF16), wd_ref[0], preferred_element_type=F32) + bd_ref[0]

    @pl.when(i >= nu_ref[0])
    def _():
        o_ref[...] = jnp.zeros(o_ref.shape, F32)


def _experts(blk_e, n_used, buf, wg, wl, wd, bg, bl, bd, *, rows):
    n_rows, d = buf.shape
    de = wg.shape[2]
    xrow = lambda i, be, nu: (i, 0)
    xin = lambda i, be, nu: (jnp.minimum(i, nu[0] - 1), 0)
    wsel = lambda i, be, nu: (be[i], 0, 0)
    grid_spec = pltpu.PrefetchScalarGridSpec(
        num_scalar_prefetch=2,
        grid=(n_rows // rows,),
        in_specs=[
            pl.BlockSpec((rows, d), xin),
            pl.BlockSpec((1, d, de), wsel),
            pl.BlockSpec((1, d, de), wsel),
            pl.BlockSpec((1, de, d), wsel),
            pl.BlockSpec((1, 1, de), wsel),
            pl.BlockSpec((1, 1, de), wsel),
            pl.BlockSpec((1, 1, d), wsel),
        ],
        out_specs=pl.BlockSpec((rows, d), xrow),
    )
    return pl.pallas_call(
        _expert_kernel,
        grid_spec=grid_spec,
        out_shape=jax.ShapeDtypeStruct((n_rows, d), F32),
        compiler_params=_cparams(("arbitrary",)),
        name="moe_experts",
    )(blk_e, n_used, buf, wg, wl, wd, bg, bl, bd)


def _combine_kernel(dest_ref, gw_ref, h_ref, obuf, y_ref, g_sc, sem, *, topk):
    tm = h_ref.shape[0]

    def issue(t, c):
        for k in range(topk):
            _row_copy(obuf, dest_ref[k, t], g_sc.at[k], t, sem).start()
        return c

    lax.fori_loop(0, tm, issue, 0, unroll=DMA_UNROLL)

    def drain(t, c):
        for k in range(topk):
            _row_copy(obuf, 0, g_sc.at[k], 0, sem).wait()
        return c

    lax.fori_loop(0, tm, drain, 0, unroll=DMA_UNROLL)
    gw = gw_ref[...]
    ff = g_sc[0] * gw[:, 0:1]
    for k in range(1, topk):
        ff = ff + g_sc[k] * gw[:, k:k + 1]
    y_ref[...] = h_ref[...] + ff


def _combine(dest, gw_t, h, obuf, *, tm, topk):
    t, d = h.shape
    return pl.pallas_call(
        functools.partial(_combine_kernel, topk=topk),
        grid=(t // tm,),
        in_specs=[
            pl.BlockSpec((topk, tm), lambda i: (0, i), memory_space=pltpu.SMEM),
            pl.BlockSpec((tm, topk), lambda i: (i, 0)),
            pl.BlockSpec((tm, d), lambda i: (i, 0)),
            pl.BlockSpec(memory_space=pl.ANY),
        ],
        out_specs=pl.BlockSpec((tm, d), lambda i: (i, 0)),
        out_shape=jax.ShapeDtypeStruct((t, d), F32),
        scratch_shapes=[pltpu.VMEM((topk, tm, d), F32), pltpu.SemaphoreType.DMA(())],
        compiler_params=_cparams(("arbitrary",)),
        name="moe_combine",
    )(dest, gw_t, h, obuf)


def _split_gu_kernel(w_ref, wg_ref, wl_ref, *, seg):
    x = w_ref[0].astype(BF16)
    src = lax.broadcasted_iota(I32, (2 * seg, 2 * seg), 0)
    dst = lax.broadcasted_iota(I32, (2 * seg, 2 * seg), 1)
    wanted = jnp.where(dst < seg, 2 * dst, 2 * (dst - seg) + 1)
    pick = (src == wanted).astype(BF16)
    for n in range(x.shape[1] // (2 * seg)):
        both = jnp.dot(x[:, n * 2 * seg:(n + 1) * 2 * seg], pick, preferred_element_type=F32)
        wg_ref[0, :, n * seg:(n + 1) * seg] = both[:, :seg].astype(BF16)
        wl_ref[0, :, n * seg:(n + 1) * seg] = both[:, seg:].astype(BF16)


def _split_gu(w_gu):
    n_exp, d, de2 = w_gu.shape
    rt = min(d, 512)
    seg = 256
    assert d % rt == 0 and de2 % (2 * seg) == 0
    return pl.pallas_call(
        functools.partial(_split_gu_kernel, seg=seg),
        grid=(n_exp, d // rt),
        in_specs=[pl.BlockSpec((1, rt, de2), lambda e, r: (e, r, 0))],
        out_specs=[pl.BlockSpec((1, rt, de2 // 2), lambda e, r: (e, r, 0))] * 2,
        out_shape=[jax.ShapeDtypeStruct((n_exp, d, de2 // 2), BF16)] * 2,
        compiler_params=_cparams(("arbitrary", "arbitrary")),
        name="split_gu",
    )(w_gu)


def _dest_kernel(ps_ref, te_ref, rk_ref, dest_ref, *, n_exp):
    te = te_ref[...]
    dest = rk_ref[...]
    for e in range(n_exp):
        dest = dest + jnp.where(te == e, ps_ref[e], 0)
    dest_ref[...] = dest


def _dest_rows(pad_start, te, rk):
    return pl.pallas_call(
        functools.partial(_dest_kernel, n_exp=pad_start.shape[0]),
        in_specs=[pl.BlockSpec(memory_space=pltpu.SMEM),
                  pl.BlockSpec(memory_space=pltpu.VMEM),
                  pl.BlockSpec(memory_space=pltpu.VMEM)],
        out_specs=pl.BlockSpec(memory_space=pltpu.VMEM),
        out_shape=jax.ShapeDtypeStruct(te.shape, I32),
        name="moe_dest",
    )(pad_start, te, rk)


def _moe(h, hn, te, gw, rk, cnt, wts, *, rows, tm_io, topk):
    t, d = h.shape
    wg, wl, wd, bg, bl, bd = wts
    n_exp = wg.shape[0]
    counts = cnt[:, 0].astype(I32)
    padded = (counts + rows - 1) // rows * rows
    pad_end = jnp.cumsum(padded)
    pad_start = pad_end - padded
    dest = _dest_rows(pad_start, te, rk)
    n_blk = -(-(t * topk) // rows) + n_exp
    blk_lo = jnp.arange(n_blk, dtype=I32) * rows
    blk_e = jnp.minimum(
        jnp.sum((pad_end[None, :] <= blk_lo[:, None]).astype(I32), axis=1), n_exp - 1)
    n_used = (pad_end[-1:] // rows).astype(I32)
    fill = jnp.stack([pad_end, padded - counts]).astype(I32)
    buf = _dispatch(fill, n_used, dest, hn, rows=rows, n_blk=n_blk, tm=tm_io, topk=topk)
    obuf = _experts(blk_e, n_used, buf, wg, wl, wd, bg, bl, bd, rows=rows)
    return _combine(dest, gw.T, h, obuf, tm=tm_io, topk=topk)


def kernel(x_prompt, x_sample, cache_k, cache_v, state_h, state_conv, page_table, norm1_g, w_in,
           q_norm_g, k_norm_g, conv_w, conv_b, w_rg, b_rg, w_ig, b_ig, lru_lambda, w_attn_o,
           w_lru_o, w_out, norm2_g, w_router, b_router, w_gu, b_gu, w_dn, b_dn):
    depth = w_in.shape[0]
    assert depth == 1, "one layer per step is supported"
    bp, sp, d = x_prompt.shape
    nbd, sd, _ = x_sample.shape
    assert sd == 1, "decode handles one new token per sequence"
    page, n_heads, head_dim = cache_k.shape[2], cache_k.shape[3], cache_k.shape[4]
    aw = n_heads * head_dim
    lw = w_lru_o.shape[1]
    n_exp = w_router.shape[2]
    n_pages = page_table.shape[1]
    past_len = n_pages * page
    blk = MOBA_BLOCK
    assert sp % blk == 0 and past_len % blk == 0 and blk % page == 0
    assert LANES % head_dim == 0 and aw % LANES == 0 and page == LANES
    ppb = blk // page
    assert past_len // blk >= MOBA_TOPK

    w_in_bf = w_in[0].astype(BF16)
    wrg_bf, wig_bf = w_rg[0].astype(BF16), w_ig[0].astype(BF16)
    wao_bf, wlo_bf, wout_bf = w_attn_o[0].astype(BF16), w_lru_o[0].astype(BF16), w_out[0].astype(BF16)
    wg_bf, wl_bf = _split_gu(w_gu[0])
    wd_bf = w_dn[0].astype(BF16)
    bg = b_gu[0][:, None, 0::2]
    bl = b_gu[0][:, None, 1::2]
    bdn = b_dn[0][:, None, :]
    moe_w = (wg_bf, wl_bf, wd_bf, bg, bl, bdn)
    qg = jnp.tile(q_norm_g[0], n_heads).reshape(1, aw)
    kg = jnp.tile(k_norm_g[0], n_heads).reshape(1, aw)
    head_of = jnp.arange(aw, dtype=I32) // head_dim
    same_head = head_of[:, None] == head_of[None, :]
    bd_mean = (same_head.astype(F32) / head_dim).astype(BF16)
    g1 = norm1_g[0].reshape(1, d)
    g2 = norm2_g[0].reshape(1, d)
    wrt = w_router[0].T.astype(BF16)
    br = b_router[0].reshape(n_exp, 1)
    cw, cb = conv_w[0], conv_b[0].reshape(1, lw)
    brg, big, lam = b_rg[0].reshape(1, lw), b_ig[0].reshape(1, lw), lru_lambda[0].reshape(1, lw)
    n_tap = cw.shape[0]

    tp = bp * sp
    cos_p, sin_p = _rope_table(sp, 0, head_dim)
    xp2 = x_prompt.reshape(tp, d)
    (q_p, k_p, v_p, kbf_p, vt_p, km_p, xb_p, gy_p, sga_p, sgl_p) = _inproj(
        xp2, g1, w_in_bf, qg, kg, cos_p, sin_p, bd_mean,
        tm=blk, n_pos_tiles=sp // blk, aw=aw, lw=lw, head_dim=head_dim, attn_aux=True)
    nb = sp // blk
    attn_p = _attention(
        q_p.reshape(bp, sp, aw), kbf_p.reshape(bp, nb, blk, aw), vt_p.reshape(bp, nb, aw, blk),
        km_p.reshape(bp, nb, aw), head_dim=head_dim, topk=MOBA_TOPK).reshape(tp, aw)
    lru_p, hlast_p = _lru_prompt(
        xb_p, gy_p, jnp.zeros((bp, n_tap - 1, lw), F32), jnp.zeros((bp, 1, lw), F32),
        cw, cb, wrg_bf, brg, wig_bf, big, lam, b=bp, s=sp, ts=blk, pos0=0)
    h_p, hn_p, te_p, gw_p, rk_p, cnt_p = _merge(
        attn_p, lru_p, sga_p, sgl_p, xp2, wao_bf, wlo_bf, wout_bf, g2, wrt, br,
        tm=blk, n_exp=n_exp, topk=TOP_K)
    y_p = _moe(h_p, hn_p, te_p, gw_p, rk_p, cnt_p, moe_w, rows=512, tm_io=blk, topk=TOP_K)
    assert sp >= n_tap - 1
    conv_p = xb_p.reshape(bp, sp, lw)[:, sp - (n_tap - 1):]

    cos_d, sin_d = _rope_table(sd, past_len, head_dim)
    cos_d = jnp.tile(cos_d, (nbd, 1))
    sin_d = jnp.tile(sin_d, (nbd, 1))
    xd2 = x_sample.reshape(nbd, d)
    (q_d, k_d, v_d, _, _, _, xb_d, gy_d, sga_d, sgl_d) = _inproj(
        xd2, g1, w_in_bf, qg, kg, cos_d, sin_d, bd_mean,
        tm=nbd, n_pos_tiles=1, aw=aw, lw=lw, head_dim=head_dim, attn_aux=False)
    ckt = cache_k[0].transpose(0, 2, 3, 1)
    cvt = cache_v[0].transpose(0, 2, 3, 1)
    q_d3 = q_d.reshape(nbd, 1, aw)
    sel = _dec_select(page_table, q_d3, ckt, ppb=ppb, blk=blk, topk=MOBA_TOPK)[:, :, :MOBA_TOPK]
    attn_d = _dec_attention(
        page_table, sel, q_d3, k_d.reshape(nbd, 1, aw), v_d.reshape(nbd, 1, aw), ckt, cvt,
        ppb=ppb, topk=MOBA_TOPK).reshape(nbd, aw)
    cs_d = state_conv[0]
    lru_d, h_d = _lru_step(xb_d, gy_d, cs_d.transpose(1, 0, 2), state_h[0], cw, cb,
                           wrg_bf, brg, wig_bf, big, lam)
    hd_, hn_d, te_d, gw_d, rk_d, cnt_d = _merge(
        attn_d, lru_d, sga_d, sgl_d, xd2, wao_bf, wlo_bf, wout_bf, g2, wrt, br,
        tm=nbd, n_exp=n_exp, topk=TOP_K)
    y_d = _moe(hd_, hn_d, te_d, gw_d, rk_d, cnt_d, moe_w, rows=2 * SUBLANES, tm_io=nbd, topk=TOP_K)
    conv_d = jnp.concatenate([cs_d, xb_d.reshape(nbd, sd, lw)], axis=1)[:, sd:]

    return (
        y_p.reshape(bp, sp, d),
        y_d.reshape(nbd, sd, d),
        k_p.reshape(1, bp, sp, n_heads, head_dim),
        v_p.reshape(1, bp, sp, n_heads, head_dim),
        hlast_p.reshape(1, bp, lw),
        conv_p.reshape(1, bp, n_tap - 1, lw),
        k_d.reshape(1, nbd, sd, n_heads, head_dim),
        v_d.reshape(1, nbd, sd, n_heads, head_dim),
        h_d.reshape(1, nbd, lw),
        conv_d.reshape(1, nbd, n_tap - 1, lw),
    )
```

```python
import functools

import jax
import jax.numpy as jnp
from jax import lax
from jax.experimental import pallas as pl
from jax.experimental.pallas import tpu as pltpu

F32 = jnp.float32
BF16 = jnp.bfloat16
I32 = jnp.int32

MOBA_BLOCK = 256
MOBA_TOPK = 3
ROPE_THETA = 10000.0
LRU_C = 8.0
TOP_K = 4
SWIGLU_LIMIT = 7.0
SWIGLU_ALPHA = 1.702
NORM_EPS = 1e-6

LANES = 128
SUBLANES = 8
VMEM_LIMIT = 56 * 1024 * 1024
DMA_UNROLL = 8
NEG = float(jnp.finfo(jnp.float32).min)
LOG2E = 1.4426950408889634
HIGHEST = lax.Precision.HIGHEST
NT = (((1,), (1,)), ((), ()))


def _cparams(sem):
    return pltpu.CompilerParams(dimension_semantics=sem, vmem_limit_bytes=VMEM_LIMIT)


def _rope_table_kernel(inv_ref, cos_ref, sin_ref, *, pos0, rows, head_dim):
    i = pl.program_id(0)
    pos = lax.broadcasted_iota(I32, (rows, LANES), 0) + (pos0 + i * rows)
    lane = lax.broadcasted_iota(I32, (rows, LANES), 1)
    ang = pos.astype(F32) * inv_ref[...]
    first_half = (lane % head_dim) < (head_dim // 2)
    cos_ref[...] = jnp.cos(ang)
    s = jnp.sin(ang)
    sin_ref[...] = jnp.where(first_half, -s, s)


def _rope_table(n_pos, pos0, head_dim):
    half = head_dim // 2
    inv = ROPE_THETA ** (-jnp.arange(half, dtype=F32) * 2.0 / head_dim)
    inv_l = jnp.tile(inv, LANES // half).reshape(1, LANES)
    n_pad = -(-n_pos // SUBLANES) * SUBLANES
    rows = min(n_pad, 512)
    assert n_pad % rows == 0
    cos, sin = pl.pallas_call(
        functools.partial(_rope_table_kernel, pos0=pos0, rows=rows, head_dim=head_dim),
        grid=(n_pad // rows,),
        in_specs=[pl.BlockSpec((1, LANES), lambda i: (0, 0))],
        out_specs=[pl.BlockSpec((rows, LANES), lambda i: (i, 0))] * 2,
        out_shape=[jax.ShapeDtypeStruct((n_pad, LANES), F32)] * 2,
        name="rope_table",
    )(inv_l)
    return cos[:n_pos], sin[:n_pos]


def _inproj_kernel(x_ref, g1_ref, w_ref, qg_ref, kg_ref, cos_ref, sin_ref, bd_ref,
                   q_ref, k_ref, v_ref, kbf_ref, vt_ref, km_ref, xb_ref, gy_ref, sga_ref, sgl_ref,
                   *, aw, lw, d, head_dim, attn_aux):
    x = x_ref[...]
    xn = x * lax.rsqrt(jnp.mean(x * x, axis=-1, keepdims=True) + NORM_EPS) * g1_ref[...]
    xn = xn.astype(BF16)
    tm = x.shape[0]

    def proj(lo, width):
        return jnp.dot(xn, w_ref[:, lo:lo + width], preferred_element_type=F32)

    reps = aw // LANES
    cos = jnp.concatenate([cos_ref[...]] * reps, axis=1)
    sin = jnp.concatenate([sin_ref[...]] * reps, axis=1)
    lane = lax.broadcasted_iota(I32, (tm, aw), 1)
    first_half = (lane % head_dim) < (head_dim // 2)
    bd = bd_ref[...]

    def head_norm_rope(t, g):
        ss = t * t
        hi = ss.astype(BF16)
        lo = (ss - hi.astype(F32)).astype(BF16)
        ms = (jnp.dot(hi, bd, preferred_element_type=F32)
              + jnp.dot(lo, bd, preferred_element_type=F32))
        tn = t * lax.rsqrt(ms + NORM_EPS) * g
        up = pltpu.roll(tn, aw - head_dim // 2, 1)
        dn = pltpu.roll(tn, head_dim // 2, 1)
        return tn * cos + jnp.where(first_half, up, dn) * sin

    q_ref[...] = head_norm_rope(proj(0, aw), qg_ref[...])
    k = head_norm_rope(proj(aw, aw), kg_ref[...])
    k_ref[...] = k
    v = proj(2 * aw, aw)
    v_ref[...] = v
    if attn_aux:
        kbf_ref[...] = k.astype(BF16)
        km_ref[0] = jnp.mean(k, axis=0, keepdims=True)
        vt_ref[0] = v.T.astype(BF16)
    else:
        kbf_ref[...] = jnp.zeros(kbf_ref.shape, BF16)
        km_ref[...] = jnp.zeros(km_ref.shape, F32)
        vt_ref[...] = jnp.zeros(vt_ref.shape, BF16)
    xb_ref[...] = proj(3 * aw, lw)
    gy_ref[...] = jax.nn.gelu(proj(3 * aw + lw, lw))
    sga_ref[...] = jax.nn.sigmoid(proj(3 * aw + 2 * lw, d))
    sgl_ref[...] = jax.nn.sigmoid(proj(3 * aw + 2 * lw + d, d))


def _inproj(x2d, g1, w_bf, qg, kg, cos, sin, bd, *, tm, n_pos_tiles, aw, lw, head_dim, attn_aux):
    t, d = x2d.shape
    nt = t // tm
    row = lambda i: (i, 0)
    const = lambda i: (0, 0)
    pos = lambda i: (i % n_pos_tiles, 0)
    outs = pl.pallas_call(
        functools.partial(_inproj_kernel, aw=aw, lw=lw, d=d, head_dim=head_dim, attn_aux=attn_aux),
        grid=(nt,),
        in_specs=[
            pl.BlockSpec((tm, d), row),
            pl.BlockSpec((1, d), const),
            pl.BlockSpec(w_bf.shape, const),
            pl.BlockSpec((1, aw), const),
            pl.BlockSpec((1, aw), const),
            pl.BlockSpec((tm, LANES), pos),
            pl.BlockSpec((tm, LANES), pos),
            pl.BlockSpec((aw, aw), const),
        ],
        out_specs=[
            pl.BlockSpec((tm, aw), row),
            pl.BlockSpec((tm, aw), row),
            pl.BlockSpec((tm, aw), row),
            pl.BlockSpec((tm, aw), row),
            pl.BlockSpec((1, aw, tm), lambda i: (i, 0, 0)),
            pl.BlockSpec((1, 1, aw), lambda i: (i, 0, 0)),
            pl.BlockSpec((tm, lw), row),
            pl.BlockSpec((tm, lw), row),
            pl.BlockSpec((tm, d), row),
            pl.BlockSpec((tm, d), row),
        ],
        out_shape=[
            jax.ShapeDtypeStruct((t, aw), F32),
            jax.ShapeDtypeStruct((t, aw), F32),
            jax.ShapeDtypeStruct((t, aw), F32),
            jax.ShapeDtypeStruct((t, aw), BF16),
            jax.ShapeDtypeStruct((nt, aw, tm), BF16),
            jax.ShapeDtypeStruct((nt, 1, aw), F32),
            jax.ShapeDtypeStruct((t, lw), F32),
            jax.ShapeDtypeStruct((t, lw), F32),
            jax.ShapeDtypeStruct((t, d), F32),
            jax.ShapeDtypeStruct((t, d), F32),
        ],
        compiler_params=_cparams(("arbitrary",)),
        name="inproj",
    )(x2d, g1, w_bf, qg, kg, cos, sin, bd)
    return outs


def _top_rows(g, n_rows, topk, valid_floor):
    row = lax.broadcasted_iota(I32, g.shape, 0)
    sel = jnp.zeros(g.shape, dtype=jnp.bool_)
    idxs = []
    vals = []
    for _ in range(topk):
        mx = jnp.max(g, axis=0, keepdims=True)
        idx = jnp.min(jnp.where(g == mx, row, n_rows), axis=0, keepdims=True)
        hit = row == idx
        sel = sel | (hit & (mx > valid_floor))
        g = jnp.where(hit, -jnp.inf, g)
        idxs.append(idx)
        vals.append(mx)
    return idxs, vals, sel


def _attn_kernel(q_ref, k_ref, vt_ref, km_ref, o_ref, sel_ref, sa_ref, sb_ref,
                 *, nb, blk, head_dim, topk, n_pairs):
    i = pl.program_id(2)
    n_h = LANES // head_dim
    n_heads = n_pairs * n_h
    lane = lax.broadcasted_iota(I32, (blk, LANES), 1)
    brow = lax.broadcasted_iota(I32, (nb, blk), 0)
    qscale = head_dim ** -0.5 * LOG2E
    qs = []
    for pp in range(n_pairs):
        lanes = slice(pp * LANES, (pp + 1) * LANES)
        q = q_ref[0, :, lanes]
        km = km_ref[0, :, lanes].astype(BF16)
        parts = []
        for hh in range(n_h):
            in_head = (lane >= hh * head_dim) & (lane < (hh + 1) * head_dim)
            qm = jnp.where(in_head, q, 0.0)
            gate = lax.dot_general(km, qm.astype(BF16), NT, preferred_element_type=F32)
            gate = jnp.where(brow < i, gate, NEG)
            _, _, sel = _top_rows(gate, nb, topk, NEG)
            sel_ref[pp * n_h + hh] = sel.astype(F32)
            parts.append((qm * qscale).astype(BF16))
        qs.append(jnp.concatenate(parts, axis=0))

    def scores(j, pp):
        return lax.dot_general(k_ref[0, j, :, pp * LANES:(pp + 1) * LANES], qs[pp], NT,
                               preferred_element_type=F32)

    def update(j, pp, hh, s, m, l, acc):
        m_new = jnp.maximum(m, jnp.max(s, axis=0, keepdims=True))
        alpha = jnp.exp2(m - m_new)
        p = jnp.exp2(s - m_new)
        l = alpha * l + jnp.sum(p, axis=0, keepdims=True)
        lo = pp * LANES + hh * head_dim
        pv = jnp.dot(vt_ref[0, j, lo:lo + head_dim, :], p.astype(BF16), preferred_element_type=F32)
        return m_new, l, alpha * acc + pv

    causal = (lax.broadcasted_iota(I32, (blk, blk), 0) <= lax.broadcasted_iota(I32, (blk, blk), 1))
    state = []
    for pp in range(n_pairs):
        s_own = scores(i, pp)
        for hh in range(n_h):
            s = jnp.where(causal, s_own[:, hh * blk:(hh + 1) * blk], NEG)
            state.extend(update(i, pp, hh, s, jnp.full((1, blk), NEG, F32),
                                jnp.zeros((1, blk), F32), jnp.zeros((head_dim, blk), F32)))

    def soft_step(j, valid, s_ref, st):
        out = []
        for pp in range(n_pairs):
            for hh in range(n_h):
                n = pp * n_h + hh
                m, l, acc = st[3 * n:3 * n + 3]
                keep = sel_ref[n, pl.ds(j, 1), :] > 0.5
                if valid is not None:
                    keep = jnp.logical_and(keep, valid)
                s = jnp.where(keep, s_ref[pp, :, hh * blk:(hh + 1) * blk], NEG)
                out.extend(update(j, pp, hh, s, m, l, acc))
        return out

    for pp in range(n_pairs):
        sa_ref[pp] = scores(0, pp)

    def body(jj, st):
        j0 = 2 * jj
        j1 = j0 + 1
        for pp in range(n_pairs):
            sb_ref[pp] = scores(j1, pp)
        st = soft_step(j0, None, sa_ref, st)
        j2 = jnp.minimum(j1 + 1, nb - 1)
        for pp in range(n_pairs):
            sa_ref[pp] = scores(j2, pp)
        return tuple(soft_step(j1, j1 < i, sb_ref, st))

    final = lax.fori_loop(0, (i + 1) // 2, body, tuple(state))
    outs = [final[3 * n + 2] / final[3 * n + 1] for n in range(n_heads)]
    o_ref[0] = jnp.concatenate(outs, axis=0).T.astype(BF16)


def _attention(q3, k4, vt4, km3, *, head_dim, topk):
    b, s, aw = q3.shape
    nb, blk = k4.shape[1], k4.shape[2]
    n_pairs = 2 if aw % (2 * LANES) == 0 else 1
    gl = n_pairs * LANES
    return pl.pallas_call(
        functools.partial(_attn_kernel, nb=nb, blk=blk, head_dim=head_dim, topk=topk,
                          n_pairs=n_pairs),
        grid=(b, aw // gl, nb),
        in_specs=[
            pl.BlockSpec((1, blk, gl), lambda bi, hp, i: (bi, i, hp)),
            pl.BlockSpec((1, nb, blk, gl), lambda bi, hp, i: (bi, 0, 0, hp)),
            pl.BlockSpec((1, nb, gl, blk), lambda bi, hp, i: (bi, 0, hp, 0)),
            pl.BlockSpec((1, nb, gl), lambda bi, hp, i: (bi, 0, hp)),
        ],
        out_specs=pl.BlockSpec((1, blk, gl), lambda bi, hp, i: (bi, i, hp)),
        out_shape=jax.ShapeDtypeStruct((b, s, aw), BF16),
        scratch_shapes=[pltpu.VMEM((gl // head_dim, nb, blk), F32),
                        pltpu.VMEM((n_pairs, blk, (LANES // head_dim) * blk), F32),
                        pltpu.VMEM((n_pairs, blk, (LANES // head_dim) * blk), F32)],
        compiler_params=_cparams(("arbitrary", "arbitrary", "arbitrary")),
        name="moba_attn",
    )(q3, k4, vt4, km3)


def _col_bcast(row):
    return jnp.broadcast_to(row, (LANES, row.shape[1])).T


def _top_lanes(g, n_cols, topk):
    col = lax.broadcasted_iota(I32, g.shape, 1)
    idxs = []
    for _ in range(topk):
        mx = jnp.max(g, axis=1, keepdims=True)
        idx = jnp.min(jnp.where(g == mx, col, n_cols), axis=1, keepdims=True)
        g = jnp.where(col == idx, -jnp.inf, g)
        idxs.append(idx)
    return idxs


def _dec_gate_kernel(pt_ref, q_ref, *refs, n_pp, ppb, n_groups, blk, topk, head_dim):
    page_refs = refs[:n_pp]
    sel_ref, qcol_sc, g_sc = refs[n_pp:]
    g = pl.program_id(1)
    n_heads = qcol_sc.shape[0]
    gw = g_sc.shape[1]
    per_step = n_pp // ppb

    @pl.when(g == 0)
    def _():
        qc = _col_bcast(q_ref[0])
        for h in range(n_heads):
            qcol_sc[h] = qc[h * head_dim:(h + 1) * head_dim]
        g_sc[...] = jnp.full(g_sc.shape, NEG, F32)

    lane = lax.broadcasted_iota(I32, (n_heads, gw), 1)
    gates = g_sc[...]
    qcol = qcol_sc[...]
    for u in range(per_step):
        tot = (page_refs[u * ppb][0] * qcol).sum(axis=1)
        for w in range(1, ppb):
            tot = tot + (page_refs[u * ppb + w][0] * qcol).sum(axis=1)
        score = jnp.sum(tot, axis=1, keepdims=True) * (1.0 / blk)
        gates = jnp.where(lane == g * per_step + u, score, gates)
    g_sc[...] = gates

    @pl.when(g == n_groups - 1)
    def _():
        idxs = _top_lanes(g_sc[...], gw, topk)
        olane = lax.broadcasted_iota(I32, (n_heads, LANES), 1)
        out = jnp.zeros((n_heads, LANES), I32)
        for r, idx in enumerate(idxs):
            out = jnp.where(olane == r, idx, out)
        sel_ref[0] = out


def _dec_select(page_table, q3, cache_kt, *, ppb, blk, topk):
    nbd, n_pages = page_table.shape
    _, n_heads, head_dim, page = cache_kt.shape
    aw = n_heads * head_dim
    n_pp = ppb
    while n_pp * 2 <= 16 and n_pages % (n_pp * 2) == 0:
        n_pp *= 2
    n_groups = n_pages // n_pp
    nbp = n_pages // ppb
    gw = -(-nbp // LANES) * LANES

    def page_spec(u):
        return pl.BlockSpec((1, n_heads, head_dim, page),
                            lambda bi, g, pt: (pt[bi, g * n_pp + u], 0, 0, 0))

    grid_spec = pltpu.PrefetchScalarGridSpec(
        num_scalar_prefetch=1,
        grid=(nbd, n_groups),
        in_specs=[pl.BlockSpec((1, 1, aw), lambda bi, g, pt: (bi, 0, 0))]
        + [page_spec(u) for u in range(n_pp)],
        out_specs=pl.BlockSpec((1, n_heads, LANES), lambda bi, g, pt: (bi, 0, 0)),
        scratch_shapes=[pltpu.VMEM((n_heads, head_dim, LANES), F32),
                        pltpu.VMEM((n_heads, gw), F32)],
    )
    return pl.pallas_call(
        functools.partial(_dec_gate_kernel, n_pp=n_pp, ppb=ppb, n_groups=n_groups, blk=blk,
                          topk=topk, head_dim=head_dim),
        grid_spec=grid_spec,
        out_shape=jax.ShapeDtypeStruct((nbd, n_heads, LANES), I32),
        compiler_params=_cparams(("arbitrary", "arbitrary")),
        name="dec_block_select",
    )(page_table, q3, *([cache_kt] * n_pp))


def _dec_attn_kernel(pt_ref, sel_ref, q_ref, kn_ref, vn_ref, *refs, n_heads, head_dim, ppb, topk):
    n_pg = n_heads * ppb
    k_refs = refs[:n_pg]
    v_refs = refs[n_pg:2 * n_pg]
    o_ref, qcol_sc, m_sc, l_sc, acc_sc = refs[2 * n_pg:]
    r = pl.program_id(1)
    scale = head_dim ** -0.5
    lane = lax.broadcasted_iota(I32, (head_dim, LANES), 1)

    @pl.when(r == 0)
    def _():
        qc = _col_bcast(q_ref[0])
        kc = _col_bcast(kn_ref[0])
        vc = _col_bcast(vn_ref[0])
        for h in range(n_heads):
            rows = slice(h * head_dim, (h + 1) * head_dim)
            qcol_sc[h] = qc[rows]
            m_sc[h:h + 1, :] = jnp.sum(qc[rows] * kc[rows], axis=0, keepdims=True) * scale
            l_sc[h:h + 1, :] = jnp.ones((1, LANES), F32)
            acc_sc[h] = jnp.where(lane == 0, vc[rows], 0.0)

    for h in range(n_heads):
        qc = qcol_sc[h]
        m = m_sc[h:h + 1, :]
        ss = [jnp.sum(k_refs[h * ppb + w][0, 0] * qc, axis=0, keepdims=True) * scale
              for w in range(ppb)]
        blk_max = ss[0]
        for s in ss[1:]:
            blk_max = jnp.maximum(blk_max, s)
        m_new = jnp.maximum(m, jnp.max(blk_max, axis=1, keepdims=True))
        alpha = jnp.exp(m - m_new)
        l = alpha * l_sc[h:h + 1, :]
        acc = alpha * acc_sc[h]
        for w in range(ppb):
            p = jnp.exp(ss[w] - m_new)
            l = l + jnp.sum(p, axis=1, keepdims=True)
            acc = acc + v_refs[h * ppb + w][0, 0] * p
        m_sc[h:h + 1, :] = m_new
        l_sc[h:h + 1, :] = l
        acc_sc[h] = acc

    @pl.when(r == topk - 1)
    def _():
        cols = [jnp.broadcast_to(
            jnp.sum(acc_sc[h], axis=1, keepdims=True) / l_sc[h:h + 1, 0:1], (head_dim, LANES))
            for h in range(n_heads)]
        o_ref[0] = jnp.concatenate(cols, axis=0).T[0:1]


def _dec_attention(page_table, sel, q3, kn3, vn3, cache_kt, cache_vt, *, ppb, topk):
    nbd = page_table.shape[0]
    _, n_heads, head_dim, page = cache_kt.shape
    aw = n_heads * head_dim

    def page_spec(h, w):
        return pl.BlockSpec(
            (1, 1, head_dim, page),
            lambda bi, r, pt, sl: (pt[bi, ppb * sl[bi, h, r] + w], h, 0, 0))

    vec = pl.BlockSpec((1, 1, aw), lambda bi, r, pt, sl: (bi, 0, 0))
    pages = [page_spec(h, w) for h in range(n_heads) for w in range(ppb)]
    grid_spec = pltpu.PrefetchScalarGridSpec(
        num_scalar_prefetch=2,
        grid=(nbd, topk),
        in_specs=[vec, vec, vec] + pages + pages,
        out_specs=vec,
        scratch_shapes=[pltpu.VMEM((n_heads, head_dim, LANES), F32),
                        pltpu.VMEM((n_heads, LANES), F32),
                        pltpu.VMEM((n_heads, LANES), F32),
                        pltpu.VMEM((n_heads, head_dim, LANES), F32)],
    )
    n_pg = n_heads * ppb
    return pl.pallas_call(
        functools.partial(_dec_attn_kernel, n_heads=n_heads, head_dim=head_dim, ppb=ppb, topk=topk),
        grid_spec=grid_spec,
        out_shape=jax.ShapeDtypeStruct((nbd, 1, aw), F32),
        compiler_params=_cparams(("arbitrary", "arbitrary")),
        name="dec_attn",
    )(page_table, sel, q3, kn3, vn3, *([cache_kt] * n_pg), *([cache_vt] * n_pg))


def _lru_gates(xc, wrg_ref, brg, wig_ref, big, lam, first_pos_mask):
    nblk, bw = wrg_ref.shape[0], wrg_ref.shape[1]
    xcb = xc.astype(BF16)
    rz = jnp.concatenate(
        [jnp.dot(xcb[:, n * bw:(n + 1) * bw], wrg_ref[n], preferred_element_type=F32)
         for n in range(nblk)], axis=1)
    iz = jnp.concatenate(
        [jnp.dot(xcb[:, n * bw:(n + 1) * bw], wig_ref[n], preferred_element_type=F32)
         for n in range(nblk)], axis=1)
    r = jax.nn.sigmoid(rz + brg)
    ig = jax.nn.sigmoid(iz + big)
    neg_lam = -lam
    softplus = jnp.maximum(neg_lam, 0.0) + jnp.log1p(jnp.exp(-jnp.abs(neg_lam)))
    log_a = -LRU_C * r * softplus
    a = jnp.exp(log_a)
    mult = jnp.sqrt(jnp.tanh(-log_a) * (a * a + 1.0))
    if first_pos_mask is not None:
        mult = jnp.where(first_pos_mask, 1.0, mult)
    return a, xc * ig * mult


def _lru_kernel(xb_ref, gy_ref, cs_ref, h0_ref, cw_ref, cb_ref, wrg_ref, brg_ref, wig_ref, big_ref,
                lam_ref, out_ref, hlast_ref, xext, a_sc, u_sc, hcar, *, ts, pos0):
    t = pl.program_id(1)
    w = xb_ref.shape[1]
    halo = SUBLANES
    n_tap = cw_ref.shape[0]

    @pl.when(t == 0)
    def _():
        xext[0:halo, :] = jnp.zeros((halo, w), F32)
        xext[halo - (n_tap - 1):halo, :] = cs_ref[0]
        hcar[...] = h0_ref[0]

    xext[halo:halo + ts, :] = xb_ref[...]
    xc = cb_ref[...]
    for j in range(n_tap):
        lo = halo - (n_tap - 1) + j
        xc = xc + xext[lo:lo + ts, :] * cw_ref[j:j + 1, :]
    xext[0:halo, :] = xext[ts:ts + halo, :]

    row = lax.broadcasted_iota(I32, (ts, w), 0)
    first = (row + t * ts + pos0) == 0
    a, u = _lru_gates(xc, wrg_ref, brg_ref[...], wig_ref, big_ref[...], lam_ref[...], first)
    a_sc[...] = a
    u_sc[...] = u

    sub = lax.broadcasted_iota(I32, (SUBLANES, LANES), 0)
    carry = hcar[...]
    ends = []
    for lc in range(w // LANES):
        cols = slice(lc * LANES, (lc + 1) * LANES)
        state = jnp.broadcast_to(carry[:, cols], (SUBLANES, LANES))
        for g2 in range(ts // (2 * SUBLANES)):
            pair = []
            for g in (2 * g2, 2 * g2 + 1):
                rows = slice(g * SUBLANES, (g + 1) * SUBLANES)
                av = a_sc[rows, cols]
                uv = u_sc[rows, cols]
                for dist in (1, 2, 4):
                    a_up = jnp.where(sub >= dist, pltpu.roll(av, dist, 0), 1.0)
                    u_up = jnp.where(sub >= dist, pltpu.roll(uv, dist, 0), 0.0)
                    uv = uv + av * u_up
                    av = av * a_up
                h = uv + av * state
                state = jnp.broadcast_to(h[SUBLANES - 1:SUBLANES, :], (SUBLANES, LANES))
                pair.append(h)
            rows2 = slice(2 * g2 * SUBLANES, (2 * g2 + 2) * SUBLANES)
            out_ref[rows2, cols] = (jnp.concatenate(pair, axis=0) * gy_ref[rows2, cols]).astype(BF16)
        ends.append(state[0:1, :])
    carry = jnp.concatenate(ends, axis=1)
    hcar[...] = carry
    hlast_ref[0] = carry


def _lru_prompt(xb, gy, cs, h0, cw, cb, wrg, brg, wig, big, lam, *, b, s, ts, pos0):
    w = xb.shape[1]
    nt = s // ts
    assert ts % (2 * SUBLANES) == 0
    n_tap = cw.shape[0]
    row = lambda bi, t: (bi * nt + t, 0)
    const2 = lambda bi, t: (0, 0)
    const3 = lambda bi, t: (0, 0, 0)
    per_b = lambda bi, t: (bi, 0, 0)
    return pl.pallas_call(
        functools.partial(_lru_kernel, ts=ts, pos0=pos0),
        grid=(b, nt),
        in_specs=[
            pl.BlockSpec((ts, w), row),
            pl.BlockSpec((ts, w), row),
            pl.BlockSpec((1, n_tap - 1, w), per_b),
            pl.BlockSpec((1, 1, w), per_b),
            pl.BlockSpec(cw.shape, const2),
            pl.BlockSpec((1, w), const2),
            pl.BlockSpec(wrg.shape, const3),
            pl.BlockSpec((1, w), const2),
            pl.BlockSpec(wig.shape, const3),
            pl.BlockSpec((1, w), const2),
            pl.BlockSpec((1, w), const2),
        ],
        out_specs=[pl.BlockSpec((ts, w), row), pl.BlockSpec((1, 1, w), per_b)],
        out_shape=[jax.ShapeDtypeStruct((b * s, w), BF16), jax.ShapeDtypeStruct((b, 1, w), F32)],
        scratch_shapes=[
            pltpu.VMEM((ts + SUBLANES, w), F32),
            pltpu.VMEM((ts, w), F32),
            pltpu.VMEM((ts, w), F32),
            pltpu.VMEM((1, w), F32),
        ],
        compiler_params=_cparams(("arbitrary", "arbitrary")),
        name="rglru_scan",
    )(xb, gy, cs, h0, cw, cb, wrg, brg, wig, big, lam)


def _lru_step_kernel(xb_ref, gy_ref, cs_ref, h0_ref, cw_ref, cb_ref, wrg_ref, brg_ref, wig_ref,
                     big_ref, lam_ref, out_ref, h_ref):
    n_tap = cw_ref.shape[0]
    xc = cb_ref[...]
    for j in range(n_tap - 1):
        xc = xc + cs_ref[j] * cw_ref[j:j + 1, :]
    xc = xc + xb_ref[...] * cw_ref[n_tap - 1:n_tap, :]
    a, u = _lru_gates(xc, wrg_ref, brg_ref[...], wig_ref, big_ref[...], lam_ref[...], None)
    h = a * h0_ref[...] + u
    h_ref[...] = h
    out_ref[...] = h * gy_ref[...]


def _lru_step(xb, gy, cs_t, h0, cw, cb, wrg, brg, wig, big, lam):
    n, w = xb.shape
    return pl.pallas_call(
        _lru_step_kernel,
        out_shape=[jax.ShapeDtypeStruct((n, w), F32), jax.ShapeDtypeStruct((n, w), F32)],
        name="rglru_step",
    )(xb, gy, cs_t, h0, cw, cb, wrg, brg, wig, big, lam)


def _merge_kernel(attn_ref, lru_ref, sga_ref, sgl_ref, x_ref, wao_ref, wlo_ref, wout_ref, g2_ref,
                  wrt_ref, br_ref, h_ref, hn_ref, te_ref, gw_ref, rk_ref, cnt_ref, carry_sc,
                  *, n_exp, topk):
    i = pl.program_id(0)
    tm = x_ref.shape[0]

    @pl.when(i == 0)
    def _():
        carry_sc[...] = jnp.zeros(carry_sc.shape, F32)

    att = jnp.dot(attn_ref[...].astype(BF16), wao_ref[...], preferred_element_type=F32)
    lru = jnp.dot(lru_ref[...].astype(BF16), wlo_ref[...], preferred_element_type=F32)
    mixed = sga_ref[...] * att + sgl_ref[...] * lru
    h = x_ref[...] + jnp.dot(mixed.astype(BF16), wout_ref[...], preferred_element_type=F32)
    h_ref[...] = h
    hn = h * lax.rsqrt(jnp.mean(h * h, axis=-1, keepdims=True) + NORM_EPS) * g2_ref[...]
    hn_ref[...] = hn

    logits = lax.dot_general(wrt_ref[...], hn.astype(BF16), NT,
                             preferred_element_type=F32) + br_ref[...]
    idxs, vals, sel = _top_rows(logits, n_exp, topk, -jnp.inf)
    top_v = jnp.concatenate(vals, axis=0)
    ex = jnp.exp(top_v - top_v[0:1])
    gw_ref[...] = ex / jnp.sum(ex, axis=0, keepdims=True)
    te_ref[...] = jnp.concatenate(idxs, axis=0)

    before = lax.broadcasted_iota(I32, (tm, tm), 0) < lax.broadcasted_iota(I32, (tm, tm), 1)
    sel_bf = sel.astype(BF16)
    prior = jnp.dot(sel_bf, before.astype(BF16), preferred_element_type=F32)
    base = prior + carry_sc[:, 0:1]
    erow = lax.broadcasted_iota(I32, (n_exp, tm), 0)
    ranks = [jnp.sum(jnp.where(erow == idx, base, 0.0), axis=0, keepdims=True) for idx in idxs]
    rk_ref[...] = jnp.concatenate(ranks, axis=0).astype(I32)
    carry_sc[...] = carry_sc[...] + jnp.sum(sel.astype(F32), axis=1, keepdims=True)
    cnt_ref[...] = carry_sc[...]


def _merge(attn, lru, sga, sgl, x2d, wao, wlo, wout, g2, wrt, br, *, tm, n_exp, topk):
    t, d = x2d.shape
    aw, lw = attn.shape[1], lru.shape[1]
    row = lambda i: (i, 0)
    col = lambda i: (0, i)
    const = lambda i: (0, 0)
    return pl.pallas_call(
        functools.partial(_merge_kernel, n_exp=n_exp, topk=topk),
        grid=(t // tm,),
        in_specs=[
            pl.BlockSpec((tm, aw), row),
            pl.BlockSpec((tm, lw), row),
            pl.BlockSpec((tm, d), row),
            pl.BlockSpec((tm, d), row),
            pl.BlockSpec((tm, d), row),
            pl.BlockSpec((aw, d), const),
            pl.BlockSpec((lw, d), const),
            pl.BlockSpec((d, d), const),
            pl.BlockSpec((1, d), const),
            pl.BlockSpec((n_exp, d), const),
            pl.BlockSpec((n_exp, 1), const),
        ],
        out_specs=[
            pl.BlockSpec((tm, d), row),
            pl.BlockSpec((tm, d), row),
            pl.BlockSpec((topk, tm), col),
            pl.BlockSpec((topk, tm), col),
            pl.BlockSpec((topk, tm), col),
            pl.BlockSpec((n_exp, LANES), const),
        ],
        out_shape=[
            jax.ShapeDtypeStruct((t, d), F32),
            jax.ShapeDtypeStruct((t, d), F32),
            jax.ShapeDtypeStruct((topk, t), I32),
            jax.ShapeDtypeStruct((topk, t), F32),
            jax.ShapeDtypeStruct((topk, t), I32),
            jax.ShapeDtypeStruct((n_exp, LANES), F32),
        ],
        scratch_shapes=[pltpu.VMEM((n_exp, LANES), F32)],
        compiler_params=_cparams(("arbitrary",)),
        name="merge_router",
    )(attn, lru, sga, sgl, x2d, wao, wlo, wout, g2, wrt, br)


def _row_copy(src, src_row, dst, dst_row, sem):
    return pltpu.make_async_copy(src.at[pl.ds(src_row, 1)], dst.at[pl.ds(dst_row, 1)], sem)


def _fill_copies(fill_ref, nu_ref, hn_ref, buf_out, sem, *, rows, n_blk):
    tm = hn_ref.shape[0]
    n_exp = fill_ref.shape[1]
    piece = min(rows, tm)
    assert rows & (rows - 1) == 0 and rows // 2 <= tm and rows % piece == 0
    assert rows % SUBLANES == 0

    def run(method):
        def per_expert(e, c):
            end, n = fill_ref[0, e], fill_ref[1, e]
            for r in range(SUBLANES - 1):
                @pl.when(r < (n & (SUBLANES - 1)))
                def _():
                    getattr(_row_copy(hn_ref, 0, buf_out, end - n + r, sem), method)()

            for b in range(SUBLANES.bit_length() - 1, rows.bit_length() - 1):
                size = 1 << b

                @pl.when(((n >> b) & 1) == 1)
                def _():
                    at = pl.multiple_of(end - ((n >> (b + 1)) << (b + 1)) - size, SUBLANES)
                    getattr(pltpu.make_async_copy(
                        hn_ref.at[pl.ds(0, size)], buf_out.at[pl.ds(at, size)], sem), method)()
            return c

        lax.fori_loop(0, n_exp, per_expert, 0)

        def per_block(bi, c):
            for p in range(rows // piece):
                getattr(pltpu.make_async_copy(
                    hn_ref.at[pl.ds(0, piece)],
                    buf_out.at[pl.ds(pl.multiple_of(bi * rows + p * piece, SUBLANES), piece)],
                    sem), method)()
            return c

        lax.fori_loop(nu_ref[0], n_blk, per_block, 0)

    return functools.partial(run, "start"), functools.partial(run, "wait")


def _dispatch_kernel(fill_ref, nu_ref, dest_ref, hn_ref, buf_out, sem, fill_sem, *, topk, rows,
                     n_blk):
    tm = hn_ref.shape[0]
    start_fill, wait_fill = _fill_copies(fill_ref, nu_ref, hn_ref, buf_out, fill_sem,
                                         rows=rows, n_blk=n_blk)

    @pl.when(pl.program_id(0) == 0)
    def _():
        start_fill()

    def issue(t, c):
        for k in range(topk):
            _row_copy(hn_ref, t, buf_out, dest_ref[k, t], sem).start()
        return c

    lax.fori_loop(0, tm, issue, 0, unroll=DMA_UNROLL)

    def drain(t, c):
        for k in range(topk):
            _row_copy(hn_ref, 0, buf_out, 0, sem).wait()
        return c

    lax.fori_loop(0, tm, drain, 0, unroll=DMA_UNROLL)

    @pl.when(pl.program_id(0) == 0)
    def _():
        wait_fill()


def _dispatch(fill, n_used, dest, hn, *, rows, n_blk, tm, topk):
    t, d = hn.shape
    grid_spec = pltpu.PrefetchScalarGridSpec(
        num_scalar_prefetch=2,
        grid=(t // tm,),
        in_specs=[
            pl.BlockSpec((topk, tm), lambda i, fl, nu: (0, i), memory_space=pltpu.SMEM),
            pl.BlockSpec((tm, d), lambda i, fl, nu: (i, 0)),
        ],
        out_specs=pl.BlockSpec(memory_space=pl.ANY),
        scratch_shapes=[pltpu.SemaphoreType.DMA(()), pltpu.SemaphoreType.DMA(())],
    )
    return pl.pallas_call(
        functools.partial(_dispatch_kernel, topk=topk, rows=rows, n_blk=n_blk),
        grid_spec=grid_spec,
        out_shape=jax.ShapeDtypeStruct((n_blk * rows, d), hn.dtype),
        compiler_params=_cparams(("arbitrary",)),
        name="moe_dispatch",
    )(fill, n_used, dest, hn)


def _expert_kernel(be_ref, nu_ref, x_ref, wg_ref, wl_ref, wd_ref, bg_ref, bl_ref, bd_ref, o_ref):
    i = pl.program_id(0)

    @pl.when(i < nu_ref[0])
    def _():
        x = x_ref[...].astype(BF16)
        g = jnp.dot(x, wg_ref[0], preferred_element_type=F32) + bg_ref[0]
        lin = jnp.dot(x, wl_ref[0], preferred_element_type=F32) + bl_ref[0]
        glu = jnp.minimum(g, SWIGLU_LIMIT)
        lin = jnp.clip(lin, -SWIGLU_LIMIT, SWIGLU_LIMIT)
        act = glu * jax.nn.sigmoid(SWIGLU_ALPHA * glu) * (lin + 1.0)
        o_ref[...] = jnp.dot(act.astype(BF16), wd_ref[0], preferred_element_type=F32) + bd_ref[0]

    @pl.when(i >= nu_ref[0])
    def _():
        o_ref[...] = jnp.zeros(o_ref.shape, F32)


def _experts(blk_e, n_used, buf, wg, wl, wd, bg, bl, bd, *, rows):
    n_rows, d = buf.shape
    de = wg.shape[2]
    xrow = lambda i, be, nu: (i, 0)
    xin = lambda i, be, nu: (jnp.minimum(i, nu[0] - 1), 0)
    wsel = lambda i, be, nu: (be[i], 0, 0)
    grid_spec = pltpu.PrefetchScalarGridSpec(
        num_scalar_prefetch=2,
        grid=(n_rows // rows,),
        in_specs=[
            pl.BlockSpec((rows, d), xin),
            pl.BlockSpec((1, d, de), wsel),
            pl.BlockSpec((1, d, de), wsel),
            pl.BlockSpec((1, de, d), wsel),
            pl.BlockSpec((1, 1, de), wsel),
            pl.BlockSpec((1, 1, de), wsel),
            pl.BlockSpec((1, 1, d), wsel),
        ],
        out_specs=pl.BlockSpec((rows, d), xrow),
    )
    return pl.pallas_call(
        _expert_kernel,
        grid_spec=grid_spec,
        out_shape=jax.ShapeDtypeStruct((n_rows, d), F32),
        compiler_params=_cparams(("arbitrary",)),
        name="moe_experts",
    )(blk_e, n_used, buf, wg, wl, wd, bg, bl, bd)


def _combine_kernel(dest_ref, gw_ref, h_ref, obuf, y_ref, g_sc, sem, *, topk):
    tm = h_ref.shape[0]

    def issue(t, c):
        for k in range(topk):
            _row_copy(obuf, dest_ref[k, t], g_sc.at[k], t, sem).start()
        return c

    lax.fori_loop(0, tm, issue, 0, unroll=DMA_UNROLL)

    def drain(t, c):
        for k in range(topk):
            _row_copy(obuf, 0, g_sc.at[k], 0, sem).wait()
        return c

    lax.fori_loop(0, tm, drain, 0, unroll=DMA_UNROLL)
    gw = gw_ref[...]
    ff = g_sc[0] * gw[:, 0:1]
    for k in range(1, topk):
        ff = ff + g_sc[k] * gw[:, k:k + 1]
    y_ref[...] = h_ref[...] + ff


def _combine(dest, gw_t, h, obuf, *, tm, topk):
    t, d = h.shape
    return pl.pallas_call(
        functools.partial(_combine_kernel, topk=topk),
        grid=(t // tm,),
        in_specs=[
            pl.BlockSpec((topk, tm), lambda i: (0, i), memory_space=pltpu.SMEM),
            pl.BlockSpec((tm, topk), lambda i: (i, 0)),
            pl.BlockSpec((tm, d), lambda i: (i, 0)),
            pl.BlockSpec(memory_space=pl.ANY),
        ],
        out_specs=pl.BlockSpec((tm, d), lambda i: (i, 0)),
        out_shape=jax.ShapeDtypeStruct((t, d), F32),
        scratch_shapes=[pltpu.VMEM((topk, tm, d), F32), pltpu.SemaphoreType.DMA(())],
        compiler_params=_cparams(("arbitrary",)),
        name="moe_combine",
    )(dest, gw_t, h, obuf)


def _split_gu_kernel(w_ref, wg_ref, wl_ref, *, seg):
    x = w_ref[0].astype(BF16)
    src = lax.broadcasted_iota(I32, (2 * seg, 2 * seg), 0)
    dst = lax.broadcasted_iota(I32, (2 * seg, 2 * seg), 1)
    wanted = jnp.where(dst < seg, 2 * dst, 2 * (dst - seg) + 1)
    pick = (src == wanted).astype(BF16)
    for n in range(x.shape[1] // (2 * seg)):
        both = jnp.dot(x[:, n * 2 * seg:(n + 1) * 2 * seg], pick, preferred_element_type=F32)
        wg_ref[0, :, n * seg:(n + 1) * seg] = both[:, :seg].astype(BF16)
        wl_ref[0, :, n * seg:(n + 1) * seg] = both[:, seg:].astype(BF16)


def _split_gu(w_gu):
    n_exp, d, de2 = w_gu.shape
    rt = min(d, 512)
    seg = 256
    assert d % rt == 0 and de2 % (2 * seg) == 0
    return pl.pallas_call(
        functools.partial(_split_gu_kernel, seg=seg),
        grid=(n_exp, d // rt),
        in_specs=[pl.BlockSpec((1, rt, de2), lambda e, r: (e, r, 0))],
        out_specs=[pl.BlockSpec((1, rt, de2 // 2), lambda e, r: (e, r, 0))] * 2,
        out_shape=[jax.ShapeDtypeStruct((n_exp, d, de2 // 2), BF16)] * 2,
        compiler_params=_cparams(("arbitrary", "arbitrary")),
        name="split_gu",
    )(w_gu)


def _dest_kernel(ps_ref, te_ref, rk_ref, dest_ref, *, n_exp):
    te = te_ref[...]
    dest = rk_ref[...]
    for e in range(n_exp):
        dest = dest + jnp.where(te == e, ps_ref[e], 0)
    dest_ref[...] = dest


def _dest_rows(pad_start, te, rk):
    return pl.pallas_call(
        functools.partial(_dest_kernel, n_exp=pad_start.shape[0]),
        in_specs=[pl.BlockSpec(memory_space=pltpu.SMEM),
                  pl.BlockSpec(memory_space=pltpu.VMEM),
                  pl.BlockSpec(memory_space=pltpu.VMEM)],
        out_specs=pl.BlockSpec(memory_space=pltpu.VMEM),
        out_shape=jax.ShapeDtypeStruct(te.shape, I32),
        name="moe_dest",
    )(pad_start, te, rk)


def _moe(h, hn, te, gw, rk, cnt, wts, *, rows, tm_io, topk):
    t, d = h.shape
    wg, wl, wd, bg, bl, bd = wts
    n_exp = wg.shape[0]
    counts = cnt[:, 0].astype(I32)
    padded = (counts + rows - 1) // rows * rows
    pad_end = jnp.cumsum(padded)
    pad_start = pad_end - padded
    dest = _dest_rows(pad_start, te, rk)
    n_blk = -(-(t * topk) // rows) + n_exp
    blk_lo = jnp.arange(n_blk, dtype=I32) * rows
    blk_e = jnp.minimum(
        jnp.sum((pad_end[None, :] <= blk_lo[:, None]).astype(I32), axis=1), n_exp - 1)
    n_used = (pad_end[-1:] // rows).astype(I32)
    fill = jnp.stack([pad_end, padded - counts]).astype(I32)
    buf = _dispatch(fill, n_used, dest, hn, rows=rows, n_blk=n_blk, tm=tm_io, topk=topk)
    obuf = _experts(blk_e, n_used, buf, wg, wl, wd, bg, bl, bd, rows=rows)
    return _combine(dest, gw.T, h, obuf, tm=tm_io, topk=topk)


def kernel(x_prompt, x_sample, cache_k, cache_v, state_h, state_conv, page_table, norm1_g, w_in,
           q_norm_g, k_norm_g, conv_w, conv_b, w_rg, b_rg, w_ig, b_ig, lru_lambda, w_attn_o,
           w_lru_o, w_out, norm2_g, w_router, b_router, w_gu, b_gu, w_dn, b_dn):
    depth = w_in.shape[0]
    assert depth == 1, "one layer per step is supported"
    bp, sp, d = x_prompt.shape
    nbd, sd, _ = x_sample.shape
    assert sd == 1, "decode handles one new token per sequence"
    page, n_heads, head_dim = cache_k.shape[2], cache_k.shape[3], cache_k.shape[4]
    aw = n_heads * head_dim
    lw = w_lru_o.shape[1]
    n_exp = w_router.shape[2]
    n_pages = page_table.shape[1]
    past_len = n_pages * page
    blk = MOBA_BLOCK
    assert sp % blk == 0 and past_len % blk == 0 and blk % page == 0
    assert LANES % head_dim == 0 and aw % LANES == 0 and page == LANES
    ppb = blk // page
    assert past_len // blk >= MOBA_TOPK

    w_in_bf = w_in[0].astype(BF16)
    wrg_bf, wig_bf = w_rg[0].astype(BF16), w_ig[0].astype(BF16)
    wao_bf, wlo_bf, wout_bf = w_attn_o[0].astype(BF16), w_lru_o[0].astype(BF16), w_out[0].astype(BF16)
    wg_bf, wl_bf = _split_gu(w_gu[0])
    wd_bf = w_dn[0].astype(BF16)
    bg = b_gu[0][:, None, 0::2]
    bl = b_gu[0][:, None, 1::2]
    bdn = b_dn[0][:, None, :]
    moe_w = (wg_bf, wl_bf, wd_bf, bg, bl, bdn)
    qg = jnp.tile(q_norm_g[0], n_heads).reshape(1, aw)
    kg = jnp.tile(k_norm_g[0], n_heads).reshape(1, aw)
    head_of = jnp.arange(aw, dtype=I32) // head_dim
    same_head = head_of[:, None] == head_of[None, :]
    bd_mean = (same_head.astype(F32) / head_dim).astype(BF16)
    g1 = norm1_g[0].reshape(1, d)
    g2 = norm2_g[0].reshape(1, d)
    wrt = w_router[0].T.astype(BF16)
    br = b_router[0].reshape(n_exp, 1)
    cw, cb = conv_w[0], conv_b[0].reshape(1, lw)
    brg, big, lam = b_rg[0].reshape(1, lw), b_ig[0].reshape(1, lw), lru_lambda[0].reshape(1, lw)
    n_tap = cw.shape[0]

    tp = bp * sp
    cos_p, sin_p = _rope_table(sp, 0, head_dim)
    xp2 = x_prompt.reshape(tp, d)
    (q_p, k_p, v_p, kbf_p, vt_p, km_p, xb_p, gy_p, sga_p, sgl_p) = _inproj(
        xp2, g1, w_in_bf, qg, kg, cos_p, sin_p, bd_mean,
        tm=blk, n_pos_tiles=sp // blk, aw=aw, lw=lw, head_dim=head_dim, attn_aux=True)
    nb = sp // blk
    attn_p = _attention(
        q_p.reshape(bp, sp, aw), kbf_p.reshape(bp, nb, blk, aw), vt_p.reshape(bp, nb, aw, blk),
        km_p.reshape(bp, nb, aw), head_dim=head_dim, topk=MOBA_TOPK).reshape(tp, aw)
    lru_p, hlast_p = _lru_prompt(
        xb_p, gy_p, jnp.zeros((bp, n_tap - 1, lw), F32), jnp.zeros((bp, 1, lw), F32),
        cw, cb, wrg_bf, brg, wig_bf, big, lam, b=bp, s=sp, ts=blk, pos0=0)
    h_p, hn_p, te_p, gw_p, rk_p, cnt_p = _merge(
        attn_p, lru_p, sga_p, sgl_p, xp2, wao_bf, wlo_bf, wout_bf, g2, wrt, br,
        tm=blk, n_exp=n_exp, topk=TOP_K)
    tm_io = next(c for c in (4 * blk, 2 * blk, blk) if tp % c == 0)
    y_p = _moe(h_p, hn_p, te_p, gw_p, rk_p, cnt_p, moe_w, rows=512, tm_io=tm_io, topk=TOP_K)
    assert sp >= n_tap - 1
    conv_p = xb_p.reshape(bp, sp, lw)[:, sp - (n_tap - 1):]

    cos_d, sin_d = _rope_table(sd, past_len, head_dim)
    cos_d = jnp.tile(cos_d, (nbd, 1))
    sin_d = jnp.tile(sin_d, (nbd, 1))
    xd2 = x_sample.reshape(nbd, d)
    (q_d, k_d, v_d, _, _, _, xb_d, gy_d, sga_d, sgl_d) = _inproj(
        xd2, g1, w_in_bf, qg, kg, cos_d, sin_d, bd_mean,
        tm=nbd, n_pos_tiles=1, aw=aw, lw=lw, head_dim=head_dim, attn_aux=False)
    ckt = cache_k[0].transpose(0, 2, 3, 1)
    cvt = cache_v[0].transpose(0, 2, 3, 1)
    q_d3 = q_d.reshape(nbd, 1, aw)
    sel = _dec_select(page_table, q_d3, ckt, ppb=ppb, blk=blk, topk=MOBA_TOPK)[:, :, :MOBA_TOPK]
    attn_d = _dec_attention(
        page_table, sel, q_d3, k_d.reshape(nbd, 1, aw), v_d.reshape(nbd, 1, aw), ckt, cvt,
        ppb=ppb, topk=MOBA_TOPK).reshape(nbd, aw)
    cs_d = state_conv[0]
    lru_d, h_d = _lru_step(xb_d, gy_d, cs_d.transpose(1, 0, 2), state_h[0], cw, cb,
                           wrg_bf, brg, wig_bf, big, lam)
    hd_, hn_d, te_d, gw_d, rk_d, cnt_d = _merge(
        attn_d, lru_d, sga_d, sgl_d, xd2, wao_bf, wlo_bf, wout_bf, g2, wrt, br,
        tm=nbd, n_exp=n_exp, topk=TOP_K)
    y_d = _moe(hd_, hn_d, te_d, gw_d, rk_d, cnt_d, moe_w, rows=2 * SUBLANES, tm_io=nbd, topk=TOP_K)
    conv_d = jnp.concatenate([cs_d, xb_d.reshape(nbd, sd, lw)], axis=1)[:, sd:]

    return (
        y_p.reshape(bp, sp, d),
        y_d.reshape(nbd, sd, d),
        k_p.reshape(1, bp, sp, n_heads, head_dim),
        v_p.reshape(1, bp, sp, n_heads, head_dim),
        hlast_p.reshape(1, bp, lw),
        conv_p.reshape(1, bp, n_tap - 1, lw),
        k_d.reshape(1, nbd, sd, n_heads, head_dim),
        v_d.reshape(1, nbd, sd, n_heads, head_dim),
        h_d.reshape(1, nbd, lw),
        conv_d.reshape(1, nbd, n_tap - 1, lw),
    )
```
